```python
import math
import jax, jax.numpy as jnp
from jax import lax
import numpy as np

D_MODEL = 2048
BATCH = 4
SEQ = 8192
DEPTH = 2
DEC_BATCH = 2
DEC_SEQ = 4096
PAST_LEN = 128

HEAD_DIM = 128
N_DIFF_HEADS = 8
DIFF_QK_DIM = HEAD_DIM // 2
N_WIN_HEADS = 8
N_WIN_KV = 2
WIN_GROUP = N_WIN_HEADS // N_WIN_KV
WINDOW = 128
BLOCK = 128
N_BUCKETS = 32
MAX_DISTANCE = 128
D_FF = 4 * D_MODEL
EPS = 1e-6
NEG = -1e30
DIFF_WIDTH = N_DIFF_HEADS * HEAD_DIM
WIN_WIDTH = N_WIN_HEADS * HEAD_DIM
MIX_WIDTH = DIFF_WIDTH + WIN_WIDTH
WIN_KV_WIDTH = N_WIN_KV * HEAD_DIM
SPLITS = (DIFF_WIDTH, DIFF_WIDTH, DIFF_WIDTH, WIN_WIDTH, WIN_KV_WIDTH, WIN_KV_WIDTH)
IN_WIDTH = sum(SPLITS)
N_HEADS_TOTAL = N_DIFF_HEADS + N_WIN_HEADS

kernel_name = "hymba_diffattn_swa_encoder"


def rmsnorm(x, g):
    xf = x.astype(jnp.float32)
    y = xf * lax.rsqrt(jnp.mean(xf * xf, axis=-1, keepdims=True) + EPS) * g.astype(jnp.float32)
    return y.astype(x.dtype)


def t5_bucket(rel):
    nb = N_BUCKETS // 2
    ret = jnp.where(rel > 0, nb, 0)
    n = jnp.abs(rel)
    max_exact = nb // 2
    nf = jnp.maximum(n, 1).astype(jnp.float32)
    large = max_exact + (jnp.log(nf / max_exact) / math.log(MAX_DISTANCE / max_exact)
                         * (nb - max_exact)).astype(jnp.int32)
    large = jnp.minimum(large, nb - 1)
    return ret + jnp.where(n < max_exact, n, large)


def diff_attention(q, k, v, lam, lambda_init, subln_g, table):
    B, S = q.shape[0], q.shape[1]
    nblk = S // BLOCK
    scale = DIFF_QK_DIM ** -0.5
    k_pos = jnp.arange(S)
    qb = q.reshape(B, nblk, BLOCK, N_DIFF_HEADS, 2, DIFF_QK_DIM).transpose(1, 0, 2, 3, 4, 5)

    def one_block(args):
        qblk, n = args
        q_pos = n * BLOCK + jnp.arange(BLOCK)
        bias = table[t5_bucket(k_pos[None, :] - q_pos[:, None])]
        bias = bias.astype(jnp.float32).transpose(2, 0, 1)[:, None]
        s = jnp.einsum('bqhcd,bkhcd->bhcqk', qblk, k).astype(jnp.float32) * scale + bias
        p = jax.nn.softmax(s, axis=-1)
        w = p[:, :, 0] - lam * p[:, :, 1]
        return jnp.einsum('bhqk,bkhe->bqhe', w.astype(v.dtype), v)

    o = lax.map(one_block, (qb, jnp.arange(nblk)))
    o = o.transpose(1, 0, 2, 3, 4).reshape(B, S, N_DIFF_HEADS, HEAD_DIM)
    o = rmsnorm(o, subln_g) * (1.0 - lambda_init)
    return o.reshape(B, S, DIFF_WIDTH)


def window_attention(q, k, v, sink, table):
    B, S = q.shape[0], q.shape[1]
    nblk = S // BLOCK
    scale = HEAD_DIM ** -0.5
    qb = q.reshape(B, nblk, BLOCK, N_WIN_KV, WIN_GROUP, HEAD_DIM)

    def band(t):
        tp = jnp.pad(t, ((0, 0), (WINDOW, WINDOW), (0, 0), (0, 0)))
        tp = tp.reshape(B, nblk + 2, BLOCK, N_WIN_KV, HEAD_DIM)
        return jnp.concatenate([tp[:, :-2], tp[:, 1:-1], tp[:, 2:]], axis=2)

    kb, vb = band(k), band(v)
    rel = jnp.arange(3 * BLOCK)[None, :] - WINDOW - jnp.arange(BLOCK)[:, None]
    key_pos = jnp.arange(nblk)[:, None] * BLOCK + jnp.arange(3 * BLOCK)[None, :] - WINDOW
    mask = (jnp.abs(rel) <= WINDOW)[None] & ((key_pos >= 0) & (key_pos < S))[:, None, :]
    bias = table[t5_bucket(rel)].astype(jnp.float32)
    bias = bias.transpose(2, 0, 1).reshape(N_WIN_KV, WIN_GROUP, BLOCK, 3 * BLOCK)
    s = jnp.einsum('bnqkgd,bnjkd->bnkgqj', qb, kb).astype(jnp.float32) * scale + bias
    s = jnp.where(mask[None, :, None, None], s, NEG)
    sink_b = sink.astype(jnp.float32).reshape(N_WIN_KV, WIN_GROUP)[None, None, :, :, None, None]
    m = jnp.maximum(jnp.max(s, axis=-1, keepdims=True), sink_b)
    e = jnp.exp(s - m)
    p = e / (jnp.sum(e, axis=-1, keepdims=True) + jnp.exp(sink_b - m))
    o = jnp.einsum('bnkgqj,bnjkd->bnqkgd', p.astype(v.dtype), vb)
    return o.reshape(B, S, WIN_WIDTH)


def trunk(x, rel_bias, norm1_g, w_in, lambda_q1, lambda_k1, lambda_q2, lambda_k2,
          diff_subln_g, sink_logit, w_out, norm2_g, w_ff_in, w_ff_out, final_norm_g):
    B, S = x.shape[0], x.shape[1]
    table_a = rel_bias[:, :N_DIFF_HEADS]
    table_b = rel_bias[:, N_DIFF_HEADS:]
    cuts = [int(c) for c in np.cumsum(SPLITS)[:-1]]
    for l in range(DEPTH):
        lambda_init = 0.8 - 0.6 * math.exp(-0.3 * l)
        lam = (jnp.exp(jnp.sum(lambda_q1[l].astype(jnp.float32) * lambda_k1[l].astype(jnp.float32)))
               - jnp.exp(jnp.sum(lambda_q2[l].astype(jnp.float32) * lambda_k2[l].astype(jnp.float32)))
               + lambda_init)
        h = rmsnorm(x, norm1_g[l])
        proj = h @ w_in[l]
        qa, ka, va, qw, kw, vw = jnp.split(proj, cuts, axis=-1)
        qa = qa.reshape(B, S, N_DIFF_HEADS, 2, DIFF_QK_DIM)
        ka = ka.reshape(B, S, N_DIFF_HEADS, 2, DIFF_QK_DIM)
        va = va.reshape(B, S, N_DIFF_HEADS, HEAD_DIM)
        qw = qw.reshape(B, S, N_WIN_HEADS, HEAD_DIM)
        kw = kw.reshape(B, S, N_WIN_KV, HEAD_DIM)
        vw = vw.reshape(B, S, N_WIN_KV, HEAD_DIM)
        oa = diff_attention(qa, ka, va, lam, lambda_init, diff_subln_g[l], table_a)
        ow = window_attention(qw, kw, vw, sink_logit[l], table_b)
        x = x + jnp.concatenate([oa, ow], axis=-1) @ w_out[l]
        h2 = rmsnorm(x, norm2_g[l])
        x = x + jnp.square(jax.nn.relu(h2 @ w_ff_in[l])) @ w_ff_out[l]
    return rmsnorm(x, final_norm_g)


def setup_inputs(seed: int = 0) -> dict:
    key = jax.random.key(seed)
    ks = jax.random.split(key, 18)
    f32 = jnp.float32
    nrm = lambda k, shape, s: jax.random.normal(k, shape, f32) * s
    return {
        "x_prompt": nrm(ks[0], (BATCH, SEQ, D_MODEL), 1.0),
        "x_sample": nrm(ks[1], (DEC_BATCH, DEC_SEQ, D_MODEL), 1.0),
        "rel_bias": nrm(ks[2], (N_BUCKETS, N_HEADS_TOTAL), 0.5),
        "norm1_g": 1.0 + nrm(ks[3], (DEPTH, D_MODEL), 0.05),
        "w_in": nrm(ks[4], (DEPTH, D_MODEL, IN_WIDTH), D_MODEL ** -0.5),
        "lambda_q1": nrm(ks[5], (DEPTH, DIFF_QK_DIM), 0.1),
        "lambda_k1": nrm(ks[6], (DEPTH, DIFF_QK_DIM), 0.1),
        "lambda_q2": nrm(ks[7], (DEPTH, DIFF_QK_DIM), 0.1),
        "lambda_k2": nrm(ks[8], (DEPTH, DIFF_QK_DIM), 0.1),
        "diff_subln_g": 1.0 + nrm(ks[9], (DEPTH, HEAD_DIM), 0.05),
        "sink_logit": nrm(ks[10], (DEPTH, N_WIN_HEADS), 0.5),
        "w_out": nrm(ks[11], (DEPTH, MIX_WIDTH, D_MODEL), MIX_WIDTH ** -0.5),
        "norm2_g": 1.0 + nrm(ks[12], (DEPTH, D_MODEL), 0.05),
        "w_ff_in": nrm(ks[13], (DEPTH, D_MODEL, D_FF), D_MODEL ** -0.5),
        "w_ff_out": nrm(ks[14], (DEPTH, D_FF, D_MODEL), D_FF ** -0.5),
        "final_norm_g": 1.0 + nrm(ks[15], (D_MODEL,), 0.05),
    }


def reference(x_prompt, x_sample, rel_bias, norm1_g, w_in, lambda_q1, lambda_k1, lambda_q2,
              lambda_k2, diff_subln_g, sink_logit, w_out, norm2_g, w_ff_in, w_ff_out, final_norm_g):
    y_prompt = trunk(x_prompt, rel_bias, norm1_g, w_in, lambda_q1, lambda_k1, lambda_q2, lambda_k2,
                     diff_subln_g, sink_logit, w_out, norm2_g, w_ff_in, w_ff_out, final_norm_g)
    y_sample = trunk(x_sample, rel_bias, norm1_g, w_in, lambda_q1, lambda_k1, lambda_q2, lambda_k2,
                     diff_subln_g, sink_logit, w_out, norm2_g, w_ff_in, w_ff_out, final_norm_g)
    return (y_prompt, y_sample)
```

```python
import functools
import math

import numpy as np
import jax
import jax.numpy as jnp
from jax import lax
from jax.experimental import pallas as pl
from jax.experimental.pallas import tpu as pltpu

D_MODEL = 2048
DEPTH = 2
HEAD_DIM = 128
N_DIFF_HEADS = 8
DIFF_QK_DIM = HEAD_DIM // 2
N_WIN_HEADS = 8
N_WIN_KV = 2
WIN_GROUP = N_WIN_HEADS // N_WIN_KV
WINDOW = 128
BLOCK = 128
N_BUCKETS = 32
MAX_DISTANCE = 128
D_FF = 4 * D_MODEL
EPS = 1e-6
NEG = -1e30
DIFF_WIDTH = N_DIFF_HEADS * HEAD_DIM
WIN_WIDTH = N_WIN_HEADS * HEAD_DIM
WIN_KV_WIDTH = N_WIN_KV * HEAD_DIM
IN_WIDTH = 3 * DIFF_WIDTH + WIN_WIDTH + 2 * WIN_KV_WIDTH

QA_BLK = 0
KA_BLK = DIFF_WIDTH // HEAD_DIM
VA_BLK = 2 * DIFF_WIDTH // HEAD_DIM
QW_COL = 3 * DIFF_WIDTH
KW_BLK = (3 * DIFF_WIDTH + WIN_WIDTH) // HEAD_DIM
VW_BLK = KW_BLK + N_WIN_KV

VMEM_LIMIT_BYTES = 56 * 1024 * 1024

ATT_TILE = 512
ROW_TILE_IN = 1024
COL_TILE_IN = 512
ROW_TILE_OUT = 512
ROW_TILE_FF = 512
FF_TILE = 512
NORM_ROWS = 128

BF16 = jnp.bfloat16
F32 = jnp.float32


def _t5_bucket_np(rel):
    nb = N_BUCKETS // 2
    max_exact = nb // 2
    ret = np.where(rel > 0, nb, 0)
    n = np.abs(rel)
    nf = np.maximum(n, 1).astype(np.float32)
    large = max_exact + (np.log(nf / np.float32(max_exact)) / np.float32(math.log(MAX_DISTANCE / max_exact))
                         * np.float32(nb - max_exact)).astype(np.int32)
    large = np.minimum(large, nb - 1)
    return (ret + np.where(n < max_exact, n, large)).astype(np.int32)


_FAR = int(np.min(np.nonzero(_t5_bucket_np(-np.arange(0, 4 * MAX_DISTANCE)) == N_BUCKETS // 2 - 1)[0]))
assert np.all(_t5_bucket_np(-np.arange(_FAR, 1 << 16)) == N_BUCKETS // 2 - 1)
assert np.all(_t5_bucket_np(np.arange(_FAR, 1 << 16)) == N_BUCKETS - 1)
assert ATT_TILE >= _FAR


def _params(semantics):
    return pltpu.CompilerParams(dimension_semantics=semantics, vmem_limit_bytes=VMEM_LIMIT_BYTES)


def _rmsnorm_rows(x, g):
    ms = jnp.mean(x * x, axis=-1, keepdims=True)
    return x * lax.rsqrt(ms + EPS) * g


def _norm_into(x_ref, g_ref, h_ref):
    rows = x_ref.shape[0]

    def body(r, carry):
        r0 = pl.multiple_of(r * NORM_ROWS, NORM_ROWS)
        x = x_ref[pl.ds(r0, NORM_ROWS), :]
        h_ref[pl.ds(r0, NORM_ROWS), :] = _rmsnorm_rows(x, g_ref[...]).astype(h_ref.dtype)
        return carry

    lax.fori_loop(0, rows // NORM_ROWS, body, 0)


def _in_proj_kernel(x_ref, g_ref, w_ref, o_ref, h_ref):
    @pl.when(pl.program_id(1) == 0)
    def _():
        _norm_into(x_ref, g_ref, h_ref)

    o_ref[...] = jnp.dot(h_ref[...], w_ref[...], preferred_element_type=F32).astype(o_ref.dtype)


def _in_proj(x, g, w):
    n_tok, d = x.shape
    n_out = w.shape[1]
    tm = min(ROW_TILE_IN, n_tok)
    return pl.pallas_call(
        _in_proj_kernel,
        grid=(n_tok // tm, n_out // COL_TILE_IN),
        in_specs=[
            pl.BlockSpec((tm, d), lambda i, j: (i, 0)),
            pl.BlockSpec((1, d), lambda i, j: (0, 0)),
            pl.BlockSpec((d, COL_TILE_IN), lambda i, j: (0, j)),
        ],
        out_specs=pl.BlockSpec((tm, COL_TILE_IN), lambda i, j: (i, j)),
        out_shape=jax.ShapeDtypeStruct((n_tok, n_out), BF16),
        scratch_shapes=[pltpu.VMEM((tm, d), BF16)],
        compiler_params=_params(("arbitrary", "arbitrary")),
        name="in_proj",
    )(x, g.reshape(1, d), w)


def _diff_attn_kernel(cfar_ref, q_ref, k_ref, v_ref, bias_ref, lq1_ref, lk1_ref, lq2_ref, lk2_ref,
                      g_ref, o_ref, vt_ref, qt_ref, m_ref, l_ref, acc_ref, *, n_chunks, lambda_init):
    t = ATT_TILE
    h = pl.program_id(1)
    qi = pl.program_id(2)

    @pl.when(qi == 0)
    def _():
        def body(j, carry):
            r0 = pl.multiple_of(j * t, t)
            vt_ref[j] = v_ref[pl.ds(r0, t), :].astype(F32).T.astype(BF16)
            return carry

        lax.fori_loop(0, n_chunks, body, 0)

    qt = q_ref[...].astype(F32).T * (DIFF_QK_DIM ** -0.5)
    row = lax.broadcasted_iota(jnp.int32, qt.shape, 0)
    qt_ref[:, :t] = jnp.where(row < DIFF_QK_DIM, qt, 0.0).astype(BF16)
    qt_ref[:, t:] = jnp.where(row >= DIFF_QK_DIM, qt, 0.0).astype(BF16)
    m_ref[...] = jnp.full(m_ref.shape, NEG, F32)
    l_ref[...] = jnp.zeros(l_ref.shape, F32)
    acc_ref[...] = jnp.zeros(acc_ref.shape, F32)

    def chunk(j, bias_tile, c):
        r0 = pl.multiple_of(j * t, t)
        s = jnp.dot(k_ref[pl.ds(r0, t), :], qt_ref[...], preferred_element_type=F32)
        if bias_tile is not None:
            bt = bias_ref[bias_tile]
            s = s + jnp.concatenate([bt, bt], axis=1)
        m_old = m_ref[...]
        m_new = jnp.maximum(m_old, jnp.max(s, axis=0, keepdims=True) + c)
        alpha = jnp.exp(m_old - m_new)
        p = jnp.exp(s - (m_new - c))
        l_ref[...] = alpha * l_ref[...] + jnp.sum(p, axis=0, keepdims=True)
        pv = jnp.dot(vt_ref[j], p.astype(BF16), preferred_element_type=F32)
        acc_ref[...] = alpha * acc_ref[...] + pv
        m_ref[...] = m_new

    c_lo = cfar_ref[0, h]
    c_hi = cfar_ref[1, h]

    def lo_body(j, carry):
        chunk(j, None, c_lo)
        return carry

    def hi_body(j, carry):
        chunk(j, None, c_hi)
        return carry

    lax.fori_loop(0, jnp.maximum(qi - 1, 0), lo_body, 0)
    for tile in range(3):
        j = qi - 1 + tile

        @pl.when(jnp.logical_and(j >= 0, j < n_chunks))
        def _(j=j, tile=tile):
            chunk(j, tile, 0.0)

    lax.fori_loop(jnp.minimum(qi + 2, n_chunks), n_chunks, hi_body, 0)

    inv_l = 1.0 / l_ref[...]
    acc = acc_ref[...]
    o0 = acc[:, :t] * inv_l[:, :t]
    o1 = acc[:, t:] * inv_l[:, t:]
    lam = (jnp.exp(jnp.sum(lq1_ref[...] * lk1_ref[...], axis=1, keepdims=True))
           - jnp.exp(jnp.sum(lq2_ref[...] * lk2_ref[...], axis=1, keepdims=True)) + lambda_init)
    ot = o0 - lam * o1
    ms = jnp.mean(ot * ot, axis=0, keepdims=True)
    y = ot * lax.rsqrt(ms + EPS) * g_ref[...] * (1.0 - lambda_init)
    o_ref[...] = y.T.astype(o_ref.dtype)


def _diff_attention(proj, batch, seq, cfar, bias_tiles, lq1, lk1, lq2, lk2, g, lambda_init):
    t = ATT_TILE
    nq = seq // t
    vec = lambda a: a.reshape(1, DIFF_QK_DIM).astype(F32)
    vec_spec = pl.BlockSpec((1, DIFF_QK_DIM), lambda b, h, qi: (0, 0))
    kernel = functools.partial(_diff_attn_kernel, n_chunks=nq, lambda_init=lambda_init)
    return pl.pallas_call(
        kernel,
        grid=(batch, N_DIFF_HEADS, nq),
        in_specs=[
            pl.BlockSpec(memory_space=pltpu.SMEM),
            pl.BlockSpec((t, HEAD_DIM), lambda b, h, qi: (b * nq + qi, QA_BLK + h)),
            pl.BlockSpec((seq, HEAD_DIM), lambda b, h, qi: (b, KA_BLK + h)),
            pl.BlockSpec((seq, HEAD_DIM), lambda b, h, qi: (b, VA_BLK + h)),
            pl.BlockSpec((None, 3, t, t), lambda b, h, qi: (h, 0, 0, 0)),
            vec_spec, vec_spec, vec_spec, vec_spec,
            pl.BlockSpec((HEAD_DIM, 1), lambda b, h, qi: (0, 0)),
        ],
        out_specs=pl.BlockSpec((t, HEAD_DIM), lambda b, h, qi: (b * nq + qi, h)),
        out_shape=jax.ShapeDtypeStruct((batch * seq, DIFF_WIDTH), BF16),
        scratch_shapes=[
            pltpu.VMEM((nq, HEAD_DIM, t), BF16),
            pltpu.VMEM((HEAD_DIM, 2 * t), BF16),
            pltpu.VMEM((1, 2 * t), F32),
            pltpu.VMEM((1, 2 * t), F32),
            pltpu.VMEM((HEAD_DIM, 2 * t), F32),
        ],
        compiler_params=_params(("arbitrary", "arbitrary", "arbitrary")),
        name="diff_attn",
    )(cfar, proj, proj, proj, bias_tiles, vec(lq1), vec(lk1), vec(lq2), vec(lk2),
      g.reshape(HEAD_DIM, 1).astype(F32))


def _win_attn_kernel(sink_ref, q_ref, kp_ref, kc_ref, kn_ref, vp_ref, vc_ref, vn_ref, bias_ref, o_ref,
                     *, seq):
    kv = pl.program_id(1)
    n = pl.program_id(2)
    kband = jnp.concatenate([kp_ref[...], kc_ref[...], kn_ref[...]], axis=0)
    vband = jnp.concatenate([vp_ref[...], vc_ref[...], vn_ref[...]], axis=0)
    col = lax.broadcasted_iota(jnp.int32, (BLOCK, 3 * BLOCK), 1)
    rowq = lax.broadcasted_iota(jnp.int32, (BLOCK, 3 * BLOCK), 0)
    rel = col - WINDOW - rowq
    key_pos = n * BLOCK + col - WINDOW
    mask = (jnp.abs(rel) <= WINDOW) & (key_pos >= 0) & (key_pos < seq)
    scale = HEAD_DIM ** -0.5
    for g in range(WIN_GROUP):
        cols = slice(g * HEAD_DIM, (g + 1) * HEAD_DIM)
        s = lax.dot_general(q_ref[:, cols], kband, (((1,), (1,)), ((), ())), preferred_element_type=F32)
        s = s * scale + bias_ref[g * BLOCK:(g + 1) * BLOCK, :]
        s = jnp.where(mask, s, NEG)
        sink = sink_ref[kv * WIN_GROUP + g]
        m = jnp.maximum(jnp.max(s, axis=-1, keepdims=True), sink)
        e = jnp.exp(s - m)
        den = jnp.sum(e, axis=-1, keepdims=True) + jnp.exp(sink - m)
        o = jnp.dot(e.astype(BF16), vband, preferred_element_type=F32) / den
        o_ref[:, cols] = o.astype(o_ref.dtype)


def _window_attention(proj, batch, seq, sink, bias_tiles):
    nblk = seq // BLOCK
    qw_blk = QW_COL // (WIN_GROUP * HEAD_DIM)
    prev = lambda b, kv, n: b * nblk + jnp.maximum(n - 1, 0)
    cur = lambda b, kv, n: b * nblk + n
    nxt = lambda b, kv, n: b * nblk + jnp.minimum(n + 1, nblk - 1)
    kspec = lambda rows, blk: pl.BlockSpec((BLOCK, HEAD_DIM), lambda b, kv, n: (rows(b, kv, n), blk + kv))
    kernel = functools.partial(_win_attn_kernel, seq=seq)
    return pl.pallas_call(
        kernel,
        grid=(batch, N_WIN_KV, nblk),
        in_specs=[
            pl.BlockSpec(memory_space=pltpu.SMEM),
            pl.BlockSpec((BLOCK, WIN_GROUP * HEAD_DIM), lambda b, kv, n: (b * nblk + n, qw_blk + kv)),
            kspec(prev, KW_BLK), kspec(cur, KW_BLK), kspec(nxt, KW_BLK),
            kspec(prev, VW_BLK), kspec(cur, VW_BLK), kspec(nxt, VW_BLK),
            pl.BlockSpec((None, WIN_GROUP * BLOCK, 3 * BLOCK), lambda b, kv, n: (kv, 0, 0)),
        ],
        out_specs=pl.BlockSpec((BLOCK, WIN_GROUP * HEAD_DIM), lambda b, kv, n: (b * nblk + n, kv)),
        out_shape=jax.ShapeDtypeStruct((batch * seq, WIN_WIDTH), BF16),
        compiler_params=_params(("arbitrary", "arbitrary", "arbitrary")),
        name="win_attn",
    )(sink.astype(F32), proj, proj, proj, proj, proj, proj, proj, bias_tiles)


def _out_proj_kernel(x_ref, oa_ref, ow_ref, wa_ref, ww_ref, o_ref):
    acc = jnp.dot(oa_ref[...], wa_ref[...], preferred_element_type=F32)
    acc = acc + jnp.dot(ow_ref[...], ww_ref[...], preferred_element_type=F32)
    o_ref[...] = x_ref[...] + acc


def _out_proj(x, oa, ow, wa, ww):
    n_tok, d = x.shape
    tm = min(ROW_TILE_OUT, n_tok)
    return pl.pallas_call(
        _out_proj_kernel,
        grid=(n_tok // tm,),
        in_specs=[
            pl.BlockSpec((tm, d), lambda i: (i, 0)),
            pl.BlockSpec((tm, DIFF_WIDTH), lambda i: (i, 0)),
            pl.BlockSpec((tm, WIN_WIDTH), lambda i: (i, 0)),
            pl.BlockSpec((DIFF_WIDTH, d), lambda i: (0, 0)),
            pl.BlockSpec((WIN_WIDTH, d), lambda i: (0, 0)),
        ],
        out_specs=pl.BlockSpec((tm, d), lambda i: (i, 0)),
        out_shape=jax.ShapeDtypeStruct((n_tok, d), F32),
        compiler_params=_params(("arbitrary",)),
        name="out_proj",
    )(x, oa, ow, wa, ww)


def _ffn_kernel(x_ref, g_ref, w1_ref, w2_ref, fg_ref, o_ref, h_ref, *, final_norm):
    f = pl.program_id(1)

    @pl.when(f == 0)
    def _():
        _norm_into(x_ref, g_ref, h_ref)

    a = jnp.dot(h_ref[...], w1_ref[...], preferred_element_type=F32)
    a = jnp.square(jnp.maximum(a, 0.0)).astype(BF16)
    contrib = jnp.dot(a, w2_ref[...], preferred_element_type=F32)

    @pl.when(f == 0)
    def _():
        o_ref[...] = x_ref[...] + contrib

    @pl.when(f > 0)
    def _():
        o_ref[...] += contrib

    if final_norm:
        @pl.when(f == pl.num_programs(1) - 1)
        def _():
            _norm_into(o_ref, fg_ref, o_ref)


def _ffn(x, g, w1, w2, final_g, final_norm):
    n_tok, d = x.shape
    d_ff = w1.shape[1]
    tm = min(ROW_TILE_FF, n_tok)
    kernel = functools.partial(_ffn_kernel, final_norm=final_norm)
    return pl.pallas_call(
        kernel,
        grid=(n_tok // tm, d_ff // FF_TILE),
        in_specs=[
            pl.BlockSpec((tm, d), lambda i, f: (i, 0)),
            pl.BlockSpec((1, d), lambda i, f: (0, 0)),
            pl.BlockSpec((d, FF_TILE), lambda i, f: (0, f)),
            pl.BlockSpec((FF_TILE, d), lambda i, f: (f, 0)),
            pl.BlockSpec((1, d), lambda i, f: (0, 0)),
        ],
        out_specs=pl.BlockSpec((tm, d), lambda i, f: (i, 0)),
        out_shape=jax.ShapeDtypeStruct((n_tok, d), F32),
        scratch_shapes=[pltpu.VMEM((tm, d), BF16)],
        compiler_params=_params(("arbitrary", "arbitrary")),
        name="ffn",
    )(x, g.reshape(1, d), w1, w2, final_g.reshape(1, d))


def _diff_bias_tiles(table_a):
    t = ATT_TILE
    k = np.arange(t)[:, None]
    q = np.arange(t)[None, :]
    bucket = np.stack([_t5_bucket_np((o - 1) * t + k - q) for o in range(3)])
    return jnp.take(table_a.T.astype(F32), jnp.asarray(bucket), axis=1)


def _win_bias_tiles(table_b):
    rel = np.arange(3 * BLOCK)[None, :] - WINDOW - np.arange(BLOCK)[:, None]
    tiles = jnp.take(table_b.T.astype(F32), jnp.asarray(_t5_bucket_np(rel)), axis=1)
    return tiles.reshape(N_WIN_KV, WIN_GROUP * BLOCK, 3 * BLOCK)


def _trunk(x3, diff_bias, win_bias, cfar, norm1_g, w_in, lambda_q1, lambda_k1, lambda_q2, lambda_k2,
           diff_subln_g, sink_logit, w_out, norm2_g, w_ff_in, w_ff_out, final_norm_g):
    batch, seq, d = x3.shape
    x = x3.reshape(batch * seq, d)
    for l in range(DEPTH):
        lambda_init = 0.8 - 0.6 * math.exp(-0.3 * l)
        proj = _in_proj(x, norm1_g[l], w_in[l])
        oa = _diff_attention(proj, batch, seq, cfar, diff_bias, lambda_q1[l], lambda_k1[l],
                             lambda_q2[l], lambda_k2[l], diff_subln_g[l], lambda_init)
        ow = _window_attention(proj, batch, seq, sink_logit[l], win_bias)
        x = _out_proj(x, oa, ow, w_out[l, :DIFF_WIDTH], w_out[l, DIFF_WIDTH:])
        x = _ffn(x, norm2_g[l], w_ff_in[l], w_ff_out[l], final_norm_g, final_norm=(l == DEPTH - 1))
    return x.reshape(batch, seq, d)


def kernel(x_prompt, x_sample, rel_bias, norm1_g, w_in, lambda_q1, lambda_k1, lambda_q2, lambda_k2,
           diff_subln_g, sink_logit, w_out, norm2_g, w_ff_in, w_ff_out, final_norm_g):
    table_a = rel_bias[:, :N_DIFF_HEADS]
    table_b = rel_bias[:, N_DIFF_HEADS:]
    diff_bias = _diff_bias_tiles(table_a)
    win_bias = _win_bias_tiles(table_b)
    cfar = jnp.stack([table_a[N_BUCKETS // 2 - 1], table_a[N_BUCKETS - 1]]).astype(F32)
    args = (diff_bias, win_bias, cfar, norm1_g, w_in.astype(BF16), lambda_q1, lambda_k1, lambda_q2,
            lambda_k2, diff_subln_g, sink_logit, w_out.astype(BF16), norm2_g, w_ff_in.astype(BF16),
            w_ff_out.astype(BF16), final_norm_g)
    return (_trunk(x_prompt, *args), _trunk(x_sample, *args))
```

```python
import functools
import math

import numpy as np
import jax
import jax.numpy as jnp
from jax import lax
from jax.experimental import pallas as pl
from jax.experimental.pallas import tpu as pltpu

D_MODEL = 2048
DEPTH = 2
HEAD_DIM = 128
N_DIFF_HEADS = 8
DIFF_QK_DIM = HEAD_DIM // 2
N_WIN_HEADS = 8
N_WIN_KV = 2
WIN_GROUP = N_WIN_HEADS // N_WIN_KV
WINDOW = 128
BLOCK = 128
N_BUCKETS = 32
MAX_DISTANCE = 128
D_FF = 4 * D_MODEL
EPS = 1e-6
NEG = -1e30
DIFF_WIDTH = N_DIFF_HEADS * HEAD_DIM
WIN_WIDTH = N_WIN_HEADS * HEAD_DIM
WIN_KV_WIDTH = N_WIN_KV * HEAD_DIM
IN_WIDTH = 3 * DIFF_WIDTH + WIN_WIDTH + 2 * WIN_KV_WIDTH

QA_BLK = 0
KA_BLK = DIFF_WIDTH // HEAD_DIM
VA_BLK = 2 * DIFF_WIDTH // HEAD_DIM
QW_COL = 3 * DIFF_WIDTH
KW_BLK = (3 * DIFF_WIDTH + WIN_WIDTH) // HEAD_DIM
VW_BLK = KW_BLK + N_WIN_KV

VMEM_LIMIT_BYTES = 56 * 1024 * 1024

LOG2E = math.log2(math.e)

ATT_TILE = 512
N_BIAS_TILES = 5
DIFF_ATTN_FLAGS = None
ROW_TILE_IN = 1024
COL_TILE_IN = 512
ROW_TILE_OUT = 512
ROW_TILE_FF = 512
FF_TILE = 512
NORM_ROWS = 128

BF16 = jnp.bfloat16
F32 = jnp.float32


def _t5_bucket_np(rel):
    nb = N_BUCKETS // 2
    max_exact = nb // 2
    ret = np.where(rel > 0, nb, 0)
    n = np.abs(rel)
    nf = np.maximum(n, 1).astype(np.float32)
    large = max_exact + (np.log(nf / np.float32(max_exact)) / np.float32(math.log(MAX_DISTANCE / max_exact))
                         * np.float32(nb - max_exact)).astype(np.int32)
    large = np.minimum(large, nb - 1)
    return (ret + np.where(n < max_exact, n, large)).astype(np.int32)


_FAR = int(np.min(np.nonzero(_t5_bucket_np(-np.arange(0, 4 * MAX_DISTANCE)) == N_BUCKETS // 2 - 1)[0]))
assert np.all(_t5_bucket_np(-np.arange(_FAR, 1 << 16)) == N_BUCKETS // 2 - 1)
assert np.all(_t5_bucket_np(np.arange(_FAR, 1 << 16)) == N_BUCKETS - 1)
assert ATT_TILE >= _FAR


def _params(semantics, flags=None):
    return pltpu.CompilerParams(dimension_semantics=semantics, vmem_limit_bytes=VMEM_LIMIT_BYTES,
                                flags=flags)


def _rmsnorm_rows(x, g):
    ms = jnp.mean(x * x, axis=-1, keepdims=True)
    return x * lax.rsqrt(ms + EPS) * g


def _norm_into(x_ref, g_ref, h_ref):
    rows = x_ref.shape[0]

    def body(r, carry):
        r0 = pl.multiple_of(r * NORM_ROWS, NORM_ROWS)
        x = x_ref[pl.ds(r0, NORM_ROWS), :]
        h_ref[pl.ds(r0, NORM_ROWS), :] = _rmsnorm_rows(x, g_ref[...]).astype(h_ref.dtype)
        return carry

    lax.fori_loop(0, rows // NORM_ROWS, body, 0)


def _in_proj_kernel(x_ref, g_ref, w_ref, cs_ref, o_ref, h_ref):
    @pl.when(pl.program_id(1) == 0)
    def _():
        _norm_into(x_ref, g_ref, h_ref)

    acc = jnp.dot(h_ref[...], w_ref[...], preferred_element_type=F32)
    o_ref[...] = (acc * cs_ref[...]).astype(o_ref.dtype)


def _in_proj_col_scale():
    cs = np.ones((1, IN_WIDTH), np.float32)
    cs[:, :DIFF_WIDTH] = LOG2E * DIFF_QK_DIM ** -0.5
    return jnp.asarray(cs)


def _in_proj(x, g, w):
    n_tok, d = x.shape
    n_out = w.shape[1]
    tm = min(ROW_TILE_IN, n_tok)
    return pl.pallas_call(
        _in_proj_kernel,
        grid=(n_tok // tm, n_out // COL_TILE_IN),
        in_specs=[
            pl.BlockSpec((tm, d), lambda i, j: (i, 0)),
            pl.BlockSpec((1, d), lambda i, j: (0, 0)),
            pl.BlockSpec((d, COL_TILE_IN), lambda i, j: (0, j)),
            pl.BlockSpec((1, COL_TILE_IN), lambda i, j: (0, j)),
        ],
        out_specs=pl.BlockSpec((tm, COL_TILE_IN), lambda i, j: (i, j)),
        out_shape=jax.ShapeDtypeStruct((n_tok, n_out), BF16),
        scratch_shapes=[pltpu.VMEM((tm, d), BF16)],
        compiler_params=_params(("arbitrary", "arbitrary")),
        name="in_proj",
    )(x, g.reshape(1, d), w, _in_proj_col_scale())


def _diff_attn_kernel(cfar_ref, q_ref, k_ref, v_ref, bias_ref, lq1_ref, lk1_ref, lq2_ref, lk2_ref,
                      g_ref, o_ref, vt_ref, qt_ref, s0_ref, s1_ref, p0_ref, p1_ref, m_ref, l_ref,
                      alpha_ref, acc_ref, *, n_chunks, lambda_init):
    t = ATT_TILE
    h = pl.program_id(1)
    qi = pl.program_id(2)
    s_refs = (s0_ref, s1_ref)
    p_refs = (p0_ref, p1_ref)
    half_cols = (slice(0, t), slice(t, 2 * t))

    @pl.when(qi == 0)
    def _():
        def body(j, carry):
            r0 = pl.multiple_of(j * t, t)
            vt_ref[j] = v_ref[pl.ds(r0, t), :].astype(F32).T.astype(BF16)
            return carry

        lax.fori_loop(0, n_chunks, body, 0)

    qt = q_ref[...].astype(F32).T
    row = lax.broadcasted_iota(jnp.int32, qt.shape, 0)
    qt_ref[:, :t] = jnp.where(row < DIFF_QK_DIM, qt, 0.0).astype(BF16)
    qt_ref[:, t:] = jnp.where(row >= DIFF_QK_DIM, qt, 0.0).astype(BF16)
    m_ref[...] = jnp.full(m_ref.shape, NEG, F32)
    l_ref[...] = jnp.zeros(l_ref.shape, F32)
    acc_ref[...] = jnp.zeros(acc_ref.shape, F32)
    alpha_ref[...] = jnp.ones(alpha_ref.shape, F32)
    p1_ref[...] = jnp.zeros(p1_ref.shape, BF16)

    def chunk_of(r):
        return lax.rem(qi + r + (n_chunks - 1), n_chunks)

    def qk(j, half):
        r0 = pl.multiple_of(j * t, t)
        s_refs[half][...] = jnp.dot(k_ref[pl.ds(r0, t), :], qt_ref[:, half_cols[half]],
                                    preferred_element_type=F32)

    def softmax(half, tile, c):
        cols = half_cols[half]

        def scores():
            s = s_refs[half][...]
            return s if tile is None else s + bias_ref[tile]

        m_old = m_ref[:, cols]
        m_new = jnp.maximum(m_old, jnp.max(scores(), axis=0, keepdims=True) + c)
        alpha = jnp.exp2(m_old - m_new)
        p = jnp.exp2(scores() - (m_new - c))
        l_ref[:, cols] = alpha * l_ref[:, cols] + jnp.sum(p, axis=0, keepdims=True)
        p_refs[half][...] = p.astype(BF16)
        alpha_ref[:, cols] = alpha
        m_ref[:, cols] = m_new

    def pv(j, half):
        cols = half_cols[half]
        acc_ref[:, cols] = alpha_ref[:, cols] * acc_ref[:, cols] + jnp.dot(
            vt_ref[j], p_refs[half][...], preferred_element_type=F32)

    def step(r, tile, c):
        j = chunk_of(r)
        pv(chunk_of(r - 1), 1)
        qk(j, 1)
        softmax(0, tile, c)
        pv(j, 0)
        qk(chunk_of(r + 1), 0)
        softmax(1, tile, c)

    c_lo = cfar_ref[0, h]
    c_hi = cfar_ref[1, h]

    def far_body(r, carry):
        step(r, None, jnp.where(chunk_of(r) < qi, c_lo, c_hi))
        return carry

    qk(chunk_of(0), 0)
    step(0, jnp.where(qi >= 1, 0, 3), 0.0)
    step(1, 1, 0.0)
    step(2, jnp.where(qi <= n_chunks - 2, 2, 4), 0.0)
    lax.fori_loop(3, n_chunks, far_body, 0)
    pv(chunk_of(n_chunks - 1), 1)

    inv_l = 1.0 / l_ref[...]
    acc = acc_ref[...]
    o0 = acc[:, :t] * inv_l[:, :t]
    o1 = acc[:, t:] * inv_l[:, t:]
    lam = (jnp.exp(jnp.sum(lq1_ref[...] * lk1_ref[...], axis=1, keepdims=True))
           - jnp.exp(jnp.sum(lq2_ref[...] * lk2_ref[...], axis=1, keepdims=True)) + lambda_init)
    ot = o0 - lam * o1
    ms = jnp.mean(ot * ot, axis=0, keepdims=True)
    y = ot * lax.rsqrt(ms + EPS) * g_ref[...] * (1.0 - lambda_init)
    o_ref[...] = y.T.astype(o_ref.dtype)


def _diff_attention(proj, batch, seq, cfar, bias_tiles, lq1, lk1, lq2, lk2, g, lambda_init):
    t = ATT_TILE
    nq = seq // t
    assert seq % t == 0 and nq >= 3, "the pipeline visits three near chunks before the far ones"
    vec = lambda a: a.reshape(1, DIFF_QK_DIM).astype(F32)
    vec_spec = pl.BlockSpec((1, DIFF_QK_DIM), lambda b, h, qi: (0, 0))
    kernel = functools.partial(_diff_attn_kernel, n_chunks=nq, lambda_init=lambda_init)
    return pl.pallas_call(
        kernel,
        grid=(batch, N_DIFF_HEADS, nq),
        in_specs=[
            pl.BlockSpec(memory_space=pltpu.SMEM),
            pl.BlockSpec((t, HEAD_DIM), lambda b, h, qi: (b * nq + qi, QA_BLK + h)),
            pl.BlockSpec((seq, HEAD_DIM), lambda b, h, qi: (b, KA_BLK + h)),
            pl.BlockSpec((seq, HEAD_DIM), lambda b, h, qi: (b, VA_BLK + h)),
            pl.BlockSpec((None, N_BIAS_TILES, t, t), lambda b, h, qi: (h, 0, 0, 0)),
            vec_spec, vec_spec, vec_spec, vec_spec,
            pl.BlockSpec((HEAD_DIM, 1), lambda b, h, qi: (0, 0)),
        ],
        out_specs=pl.BlockSpec((t, HEAD_DIM), lambda b, h, qi: (b * nq + qi, h)),
        out_shape=jax.ShapeDtypeStruct((batch * seq, DIFF_WIDTH), BF16),
        scratch_shapes=[
            pltpu.VMEM((nq, HEAD_DIM, t), BF16),
            pltpu.VMEM((HEAD_DIM, 2 * t), BF16),
            pltpu.VMEM((t, t), F32),
            pltpu.VMEM((t, t), F32),
            pltpu.VMEM((t, t), BF16),
            pltpu.VMEM((t, t), BF16),
            pltpu.VMEM((1, 2 * t), F32),
            pltpu.VMEM((1, 2 * t), F32),
            pltpu.VMEM((1, 2 * t), F32),
            pltpu.VMEM((HEAD_DIM, 2 * t), F32),
        ],
        compiler_params=_params(("arbitrary", "arbitrary", "arbitrary"), DIFF_ATTN_FLAGS),
        name="diff_attn",
    )(cfar, proj, proj, proj, bias_tiles, vec(lq1), vec(lk1), vec(lq2), vec(lk2),
      g.reshape(HEAD_DIM, 1).astype(F32))


def _win_attn_kernel(sink_ref, q_ref, kp_ref, kc_ref, kn_ref, vp_ref, vc_ref, vn_ref, bias_ref, o_ref,
                     *, seq):
    kv = pl.program_id(1)
    n = pl.program_id(2)
    kband = jnp.concatenate([kp_ref[...], kc_ref[...], kn_ref[...]], axis=0)
    vband = jnp.concatenate([vp_ref[...], vc_ref[...], vn_ref[...]], axis=0)
    col = lax.broadcasted_iota(jnp.int32, (BLOCK, 3 * BLOCK), 1)
    rowq = lax.broadcasted_iota(jnp.int32, (BLOCK, 3 * BLOCK), 0)
    rel = col - WINDOW - rowq
    key_pos = n * BLOCK + col - WINDOW
    mask = (jnp.abs(rel) <= WINDOW) & (key_pos >= 0) & (key_pos < seq)
    scale = HEAD_DIM ** -0.5
    for g in range(WIN_GROUP):
        cols = slice(g * HEAD_DIM, (g + 1) * HEAD_DIM)
        s = lax.dot_general(q_ref[:, cols], kband, (((1,), (1,)), ((), ())), preferred_element_type=F32)
        s = s * scale + bias_ref[g * BLOCK:(g + 1) * BLOCK, :]
        s = jnp.where(mask, s, NEG)
        sink = sink_ref[kv * WIN_GROUP + g]
        m = jnp.maximum(jnp.max(s, axis=-1, keepdims=True), sink)
        e = jnp.exp(s - m)
        den = jnp.sum(e, axis=-1, keepdims=True) + jnp.exp(sink - m)
        o = jnp.dot(e.astype(BF16), vband, preferred_element_type=F32) / den
        o_ref[:, cols] = o.astype(o_ref.dtype)


def _window_attention(proj, batch, seq, sink, bias_tiles):
    nblk = seq // BLOCK
    qw_blk = QW_COL // (WIN_GROUP * HEAD_DIM)
    prev = lambda b, kv, n: b * nblk + jnp.maximum(n - 1, 0)
    cur = lambda b, kv, n: b * nblk + n
    nxt = lambda b, kv, n: b * nblk + jnp.minimum(n + 1, nblk - 1)
    kspec = lambda rows, blk: pl.BlockSpec((BLOCK, HEAD_DIM), lambda b, kv, n: (rows(b, kv, n), blk + kv))
    kernel = functools.partial(_win_attn_kernel, seq=seq)
    return pl.pallas_call(
        kernel,
        grid=(batch, N_WIN_KV, nblk),
        in_specs=[
            pl.BlockSpec(memory_space=pltpu.SMEM),
            pl.BlockSpec((BLOCK, WIN_GROUP * HEAD_DIM), lambda b, kv, n: (b * nblk + n, qw_blk + kv)),
            kspec(prev, KW_BLK), kspec(cur, KW_BLK), kspec(nxt, KW_BLK),
            kspec(prev, VW_BLK), kspec(cur, VW_BLK), kspec(nxt, VW_BLK),
            pl.BlockSpec((None, WIN_GROUP * BLOCK, 3 * BLOCK), lambda b, kv, n: (kv, 0, 0)),
        ],
        out_specs=pl.BlockSpec((BLOCK, WIN_GROUP * HEAD_DIM), lambda b, kv, n: (b * nblk + n, kv)),
        out_shape=jax.ShapeDtypeStruct((batch * seq, WIN_WIDTH), BF16),
        compiler_params=_params(("arbitrary", "arbitrary", "arbitrary")),
        name="win_attn",
    )(sink.astype(F32), proj, proj, proj, proj, proj, proj, proj, bias_tiles)


def _out_proj_kernel(x_ref, oa_ref, ow_ref, wa_ref, ww_ref, o_ref):
    acc = jnp.dot(oa_ref[...], wa_ref[...], preferred_element_type=F32)
    acc = acc + jnp.dot(ow_ref[...], ww_ref[...], preferred_element_type=F32)
    o_ref[...] = x_ref[...] + acc


def _out_proj(x, oa, ow, wa, ww):
    n_tok, d = x.shape
    tm = min(ROW_TILE_OUT, n_tok)
    return pl.pallas_call(
        _out_proj_kernel,
        grid=(n_tok // tm,),
        in_specs=[
            pl.BlockSpec((tm, d), lambda i: (i, 0)),
            pl.BlockSpec((tm, DIFF_WIDTH), lambda i: (i, 0)),
            pl.BlockSpec((tm, WIN_WIDTH), lambda i: (i, 0)),
            pl.BlockSpec((DIFF_WIDTH, d), lambda i: (0, 0)),
            pl.BlockSpec((WIN_WIDTH, d), lambda i: (0, 0)),
        ],
        out_specs=pl.BlockSpec((tm, d), lambda i: (i, 0)),
        out_shape=jax.ShapeDtypeStruct((n_tok, d), F32),
        compiler_params=_params(("arbitrary",)),
        name="out_proj",
    )(x, oa, ow, wa, ww)


def _ffn_kernel(x_ref, g_ref, w1_ref, w2_ref, fg_ref, o_ref, h_ref, *, final_norm):
    f = pl.program_id(1)

    @pl.when(f == 0)
    def _():
        _norm_into(x_ref, g_ref, h_ref)

    a = jnp.dot(h_ref[...], w1_ref[...], preferred_element_type=F32)
    a = jnp.square(jnp.maximum(a, 0.0)).astype(BF16)
    contrib = jnp.dot(a, w2_ref[...], preferred_element_type=F32)

    @pl.when(f == 0)
    def _():
        o_ref[...] = x_ref[...] + contrib

    @pl.when(f > 0)
    def _():
        o_ref[...] += contrib

    if final_norm:
        @pl.when(f == pl.num_programs(1) - 1)
        def _():
            _norm_into(o_ref, fg_ref, o_ref)


def _ffn(x, g, w1, w2, final_g, final_norm):
    n_tok, d = x.shape
    d_ff = w1.shape[1]
    tm = min(ROW_TILE_FF, n_tok)
    kernel = functools.partial(_ffn_kernel, final_norm=final_norm)
    return pl.pallas_call(
        kernel,
        grid=(n_tok // tm, d_ff // FF_TILE),
        in_specs=[
            pl.BlockSpec((tm, d), lambda i, f: (i, 0)),
            pl.BlockSpec((1, d), lambda i, f: (0, 0)),
            pl.BlockSpec((d, FF_TILE), lambda i, f: (0, f)),
            pl.BlockSpec((FF_TILE, d), lambda i, f: (f, 0)),
            pl.BlockSpec((1, d), lambda i, f: (0, 0)),
        ],
        out_specs=pl.BlockSpec((tm, d), lambda i, f: (i, 0)),
        out_shape=jax.ShapeDtypeStruct((n_tok, d), F32),
        scratch_shapes=[pltpu.VMEM((tm, d), BF16)],
        compiler_params=_params(("arbitrary", "arbitrary")),
        name="ffn",
    )(x, g.reshape(1, d), w1, w2, final_g.reshape(1, d))


def _bucket_lookup(table, bucket):
    bucket = jnp.asarray(bucket.astype(np.int8))
    cols = table.T.astype(F32).reshape((table.shape[1],) + (1,) * bucket.ndim + (N_BUCKETS,))
    out = jnp.zeros((table.shape[1],) + bucket.shape, F32)
    for b in range(N_BUCKETS):
        out = jnp.where(bucket == b, cols[..., b], out)
    return out


def _diff_bias_tiles(table_a):
    t = ATT_TILE
    k = np.arange(t)[:, None]
    q = np.arange(t)[None, :]
    bucket = np.stack([_t5_bucket_np((o - 1) * t + k - q) for o in range(3)]
                      + [np.full((t, t), N_BUCKETS - 1), np.full((t, t), N_BUCKETS // 2 - 1)])
    assert bucket.shape[0] == N_BIAS_TILES
    return _bucket_lookup(table_a * LOG2E, bucket)


def _win_bias_tiles(table_b):
    rel = np.arange(3 * BLOCK)[None, :] - WINDOW - np.arange(BLOCK)[:, None]
    tiles = _bucket_lookup(table_b, _t5_bucket_np(rel))
    return tiles.reshape(N_WIN_KV, WIN_GROUP * BLOCK, 3 * BLOCK)


def _trunk(x3, diff_bias, win_bias, cfar, norm1_g, w_in, lambda_q1, lambda_k1, lambda_q2, lambda_k2,
           diff_subln_g, sink_logit, w_out, norm2_g, w_ff_in, w_ff_out, final_norm_g):
    batch, seq, d = x3.shape
    x = x3.reshape(batch * seq, d)
    for l in range(DEPTH):
        lambda_init = 0.8 - 0.6 * math.exp(-0.3 * l)
        proj = _in_proj(x, norm1_g[l], w_in[l])
        oa = _diff_attention(proj, batch, seq, cfar, diff_bias, lambda_q1[l], lambda_k1[l],
                             lambda_q2[l], lambda_k2[l], diff_subln_g[l], lambda_init)
        ow = _window_attention(proj, batch, seq, sink_logit[l], win_bias)
        x = _out_proj(x, oa, ow, w_out[l, :DIFF_WIDTH], w_out[l, DIFF_WIDTH:])
        x = _ffn(x, norm2_g[l], w_ff_in[l], w_ff_out[l], final_norm_g, final_norm=(l == DEPTH - 1))
    return x.reshape(batch, seq, d)


def kernel(x_prompt, x_sample, rel_bias, norm1_g, w_in, lambda_q1, lambda_k1, lambda_q2, lambda_k2,
           diff_subln_g, sink_logit, w_out, norm2_g, w_ff_in, w_ff_out, final_norm_g):
    table_a = rel_bias[:, :N_DIFF_HEADS]
    table_b = rel_bias[:, N_DIFF_HEADS:]
    diff_bias = _diff_bias_tiles(table_a)
    win_bias = _win_bias_tiles(table_b)
    cfar = jnp.stack([table_a[N_BUCKETS // 2 - 1], table_a[N_BUCKETS - 1]]).astype(F32) * LOG2E
    args = (diff_bias, win_bias, cfar, norm1_g, w_in.astype(BF16), lambda_q1, lambda_k1, lambda_q2,
            lambda_k2, diff_subln_g, sink_logit, w_out.astype(BF16), norm2_g, w_ff_in.astype(BF16),
            w_ff_out.astype(BF16), final_norm_g)
    return (_trunk(x_prompt, *args), _trunk(x_sample, *args))
```

```python
import functools
import math

import numpy as np
import jax
import jax.numpy as jnp
from jax import lax
from jax.experimental import pallas as pl
from jax.experimental.pallas import tpu as pltpu

D_MODEL = 2048
DEPTH = 2
HEAD_DIM = 128
N_DIFF_HEADS = 8
DIFF_QK_DIM = HEAD_DIM // 2
N_WIN_HEADS = 8
N_WIN_KV = 2
WIN_GROUP = N_WIN_HEADS // N_WIN_KV
WINDOW = 128
BLOCK = 128
N_BUCKETS = 32
MAX_DISTANCE = 128
D_FF = 4 * D_MODEL
EPS = 1e-6
NEG = -1e30
DIFF_WIDTH = N_DIFF_HEADS * HEAD_DIM
WIN_WIDTH = N_WIN_HEADS * HEAD_DIM
WIN_KV_WIDTH = N_WIN_KV * HEAD_DIM
IN_WIDTH = 3 * DIFF_WIDTH + WIN_WIDTH + 2 * WIN_KV_WIDTH

QA_BLK = 0
KA_BLK = DIFF_WIDTH // HEAD_DIM
VA_BLK = 2 * DIFF_WIDTH // HEAD_DIM
QW_COL = 3 * DIFF_WIDTH
KW_BLK = (3 * DIFF_WIDTH + WIN_WIDTH) // HEAD_DIM
VW_BLK = KW_BLK + N_WIN_KV

VMEM_LIMIT_BYTES = 56 * 1024 * 1024

LOG2E = math.log2(math.e)

ATT_TILE = 512
N_BIAS_TILES = 5
SUM_ROWS = 16
DIFF_ATTN_FLAGS = None
ROW_TILE_IN = 1024
COL_TILE_IN = 512
ROW_TILE_OUT = 512
ROW_TILE_FF = 512
FF_TILE = 512
FF_ROW_GROUPS = 2
NORM_ROWS = 128

BF16 = jnp.bfloat16
F32 = jnp.float32


def _t5_bucket_np(rel):
    nb = N_BUCKETS // 2
    max_exact = nb // 2
    ret = np.where(rel > 0, nb, 0)
    n = np.abs(rel)
    nf = np.maximum(n, 1).astype(np.float32)
    large = max_exact + (np.log(nf / np.float32(max_exact)) / np.float32(math.log(MAX_DISTANCE / max_exact))
                         * np.float32(nb - max_exact)).astype(np.int32)
    large = np.minimum(large, nb - 1)
    return (ret + np.where(n < max_exact, n, large)).astype(np.int32)


_FAR = int(np.min(np.nonzero(_t5_bucket_np(-np.arange(0, 4 * MAX_DISTANCE)) == N_BUCKETS // 2 - 1)[0]))
assert np.all(_t5_bucket_np(-np.arange(_FAR, 1 << 16)) == N_BUCKETS // 2 - 1)
assert np.all(_t5_bucket_np(np.arange(_FAR, 1 << 16)) == N_BUCKETS - 1)
assert ATT_TILE >= _FAR


def _params(semantics, flags=None):
    return pltpu.CompilerParams(dimension_semantics=semantics, vmem_limit_bytes=VMEM_LIMIT_BYTES,
                                flags=flags)


def _rmsnorm_rows(x, g):
    ms = jnp.mean(x * x, axis=-1, keepdims=True)
    return x * lax.rsqrt(ms + EPS) * g


def _norm_into(x_ref, g_ref, h_ref, copy_ref=None):
    rows = x_ref.shape[0]

    def body(r, carry):
        r0 = pl.multiple_of(r * NORM_ROWS, NORM_ROWS)
        x = x_ref[pl.ds(r0, NORM_ROWS), :]
        h_ref[pl.ds(r0, NORM_ROWS), :] = _rmsnorm_rows(x, g_ref[...]).astype(h_ref.dtype)
        if copy_ref is not None:
            copy_ref[pl.ds(r0, NORM_ROWS), :] = x
        return carry

    lax.fori_loop(0, rows // NORM_ROWS, body, 0)


def _in_proj_kernel(x_ref, g_ref, w_ref, cs_ref, o_ref, h_ref):
    @pl.when(pl.program_id(1) == 0)
    def _():
        _norm_into(x_ref, g_ref, h_ref)

    acc = jnp.dot(h_ref[...], w_ref[...], preferred_element_type=F32)
    o_ref[...] = (acc * cs_ref[...]).astype(o_ref.dtype)


def _in_proj_col_scale():
    cs = np.ones((1, IN_WIDTH), np.float32)
    cs[:, :DIFF_WIDTH] = LOG2E * DIFF_QK_DIM ** -0.5
    return jnp.asarray(cs)


def _in_proj(x, g, w):
    n_tok, d = x.shape
    n_out = w.shape[1]
    tm = min(ROW_TILE_IN, n_tok)
    return pl.pallas_call(
        _in_proj_kernel,
        grid=(n_tok // tm, n_out // COL_TILE_IN),
        in_specs=[
            pl.BlockSpec((tm, d), lambda i, j: (i, 0)),
            pl.BlockSpec((1, d), lambda i, j: (0, 0)),
            pl.BlockSpec((d, COL_TILE_IN), lambda i, j: (0, j)),
            pl.BlockSpec((1, COL_TILE_IN), lambda i, j: (0, j)),
        ],
        out_specs=pl.BlockSpec((tm, COL_TILE_IN), lambda i, j: (i, j)),
        out_shape=jax.ShapeDtypeStruct((n_tok, n_out), BF16),
        scratch_shapes=[pltpu.VMEM((tm, d), BF16)],
        compiler_params=_params(("arbitrary", "arbitrary")),
        name="in_proj",
    )(x, g.reshape(1, d), w, _in_proj_col_scale())


def _diff_attn_kernel(cfar_ref, zero_ref, q_ref, k_ref, v_ref, bias_ref, lq1_ref, lk1_ref, lq2_ref, lk2_ref,
                      g_ref, o_ref, vt_ref, qt_ref, s0_ref, s1_ref, p0_ref, p1_ref, m_ref,
                      alpha_ref, acc_ref, *, n_chunks, lambda_init):
    t = ATT_TILE
    h = pl.program_id(1)
    qi = pl.program_id(2)
    s_refs = (s0_ref, s1_ref)
    p_refs = (p0_ref, p1_ref)
    half_cols = (slice(0, t), slice(t, 2 * t))

    @pl.when(qi == 0)
    def _():
        def body(j, carry):
            r0 = pl.multiple_of(j * t, t)
            vt_ref[j, :HEAD_DIM] = v_ref[pl.ds(r0, t), :].astype(F32).T.astype(BF16)
            vt_ref[j, HEAD_DIM:] = jnp.ones((SUM_ROWS, t), BF16)
            return carry

        lax.fori_loop(0, n_chunks, body, 0)

    qt = q_ref[...].astype(F32).T
    row = lax.broadcasted_iota(jnp.int32, qt.shape, 0)
    qt_ref[:, :t] = jnp.where(row < DIFF_QK_DIM, qt, 0.0).astype(BF16)
    qt_ref[:, t:] = jnp.where(row >= DIFF_QK_DIM, qt, 0.0).astype(BF16)
    m_ref[...] = jnp.full(m_ref.shape, NEG, F32)
    acc_ref[...] = jnp.zeros(acc_ref.shape, F32)
    alpha_ref[...] = jnp.ones(alpha_ref.shape, F32)
    p1_ref[...] = jnp.zeros(p1_ref.shape, BF16)

    def chunk_of(r):
        return lax.rem(qi + r + (n_chunks - 1), n_chunks)

    def qk(j, half):
        r0 = pl.multiple_of(j * t, t)
        s_refs[half][...] = jnp.dot(k_ref[pl.ds(r0, t), :], qt_ref[:, half_cols[half]],
                                    preferred_element_type=F32)

    def softmax(half, tile, c):
        cols = half_cols[half]

        def scores(row0):
            s = s_refs[half][pl.ds(row0, t), :]
            return s if tile is None else s + bias_ref[tile]

        m_old = m_ref[:, cols]
        m_new = jnp.maximum(m_old, jnp.max(scores(0), axis=0, keepdims=True) + c)
        alpha = jnp.exp2(m_old - m_new)
        p = jnp.exp2(scores(pl.multiple_of(zero_ref[0], t)) - (m_new - c))
        p_refs[half][...] = p.astype(BF16)
        alpha_ref[:, cols] = alpha
        m_ref[:, cols] = m_new

    def pv(j, half):
        cols = half_cols[half]
        acc_ref[:, cols] = alpha_ref[:, cols] * acc_ref[:, cols] + jnp.dot(
            vt_ref[j], p_refs[half][...], preferred_element_type=F32)

    def step(r, tile, c):
        j = chunk_of(r)
        pv(chunk_of(r - 1), 1)
        qk(j, 1)
        softmax(0, tile, c)
        pv(j, 0)
        qk(chunk_of(r + 1), 0)
        softmax(1, tile, c)

    c_lo = cfar_ref[0, h]
    c_hi = cfar_ref[1, h]

    def far_step(r):
        step(r, None, jnp.where(chunk_of(r) < qi, c_lo, c_hi))

    def far_pair(i, carry):
        far_step(4 + 2 * i)
        far_step(5 + 2 * i)
        return carry

    qk(chunk_of(0), 0)
    step(0, jnp.where(qi >= 1, 0, 3), 0.0)
    step(1, 1, 0.0)
    step(2, jnp.where(qi <= n_chunks - 2, 2, 4), 0.0)
    far_step(3)
    lax.fori_loop(0, (n_chunks - 4) // 2, far_pair, 0)
    pv(chunk_of(n_chunks - 1), 1)

    inv_l = 1.0 / acc_ref[HEAD_DIM:HEAD_DIM + 1, :]
    acc = acc_ref[:HEAD_DIM, :]
    o0 = acc[:, :t] * inv_l[:, :t]
    o1 = acc[:, t:] * inv_l[:, t:]
    lam = (jnp.exp(jnp.sum(lq1_ref[...] * lk1_ref[...], axis=1, keepdims=True))
           - jnp.exp(jnp.sum(lq2_ref[...] * lk2_ref[...], axis=1, keepdims=True)) + lambda_init)
    ot = o0 - lam * o1
    ms = jnp.mean(ot * ot, axis=0, keepdims=True)
    y = ot * lax.rsqrt(ms + EPS) * g_ref[...] * (1.0 - lambda_init)
    o_ref[...] = y.T.astype(o_ref.dtype)


def _diff_attention(proj, batch, seq, cfar, bias_tiles, lq1, lk1, lq2, lk2, g, lambda_init):
    t = ATT_TILE
    nq = seq // t
    assert seq % t == 0 and nq >= 4 and nq % 2 == 0, "three near chunks, then far chunks in pairs"
    vec = lambda a: a.reshape(1, DIFF_QK_DIM).astype(F32)
    vec_spec = pl.BlockSpec((1, DIFF_QK_DIM), lambda b, h, qi: (0, 0))
    kernel = functools.partial(_diff_attn_kernel, n_chunks=nq, lambda_init=lambda_init)
    return pl.pallas_call(
        kernel,
        grid=(batch, N_DIFF_HEADS, nq),
        in_specs=[
            pl.BlockSpec(memory_space=pltpu.SMEM),
            pl.BlockSpec(memory_space=pltpu.SMEM),
            pl.BlockSpec((t, HEAD_DIM), lambda b, h, qi: (b * nq + qi, QA_BLK + h)),
            pl.BlockSpec((seq, HEAD_DIM), lambda b, h, qi: (b, KA_BLK + h)),
            pl.BlockSpec((seq, HEAD_DIM), lambda b, h, qi: (b, VA_BLK + h)),
            pl.BlockSpec((None, N_BIAS_TILES, t, t), lambda b, h, qi: (h, 0, 0, 0)),
            vec_spec, vec_spec, vec_spec, vec_spec,
            pl.BlockSpec((HEAD_DIM, 1), lambda b, h, qi: (0, 0)),
        ],
        out_specs=pl.BlockSpec((t, HEAD_DIM), lambda b, h, qi: (b * nq + qi, h)),
        out_shape=jax.ShapeDtypeStruct((batch * seq, DIFF_WIDTH), BF16),
        scratch_shapes=[
            pltpu.VMEM((nq, HEAD_DIM + SUM_ROWS, t), BF16),
            pltpu.VMEM((HEAD_DIM, 2 * t), BF16),
            pltpu.VMEM((t, t), F32),
            pltpu.VMEM((t, t), F32),
            pltpu.VMEM((t, t), BF16),
            pltpu.VMEM((t, t), BF16),
            pltpu.VMEM((1, 2 * t), F32),
            pltpu.VMEM((1, 2 * t), F32),
            pltpu.VMEM((HEAD_DIM + SUM_ROWS, 2 * t), F32),
        ],
        compiler_params=_params(("arbitrary", "arbitrary", "arbitrary"), DIFF_ATTN_FLAGS),
        name="diff_attn",
    )(cfar, jnp.zeros((1,), jnp.int32), proj, proj, proj, bias_tiles, vec(lq1), vec(lk1), vec(lq2), vec(lk2),
      g.reshape(HEAD_DIM, 1).astype(F32))


def _win_attn_kernel(sink_ref, q_ref, kp_ref, kc_ref, kn_ref, vp_ref, vc_ref, vn_ref, bias_ref, o_ref,
                     *, seq):
    n = pl.program_id(1)
    key_pos = n * BLOCK - WINDOW + lax.broadcasted_iota(jnp.int32, (1, 3 * BLOCK), 1)
    mask = (key_pos >= 0) & (key_pos < seq)

    def band(prev_ref, cur_ref, next_ref, kv):
        cols = slice(kv * HEAD_DIM, (kv + 1) * HEAD_DIM)
        return jnp.concatenate([prev_ref[:, cols], cur_ref[:, cols], next_ref[:, cols]], axis=0)

    def head_cols(kv, g):
        h = kv * WIN_GROUP + g
        return slice(h * HEAD_DIM, (h + 1) * HEAD_DIM)

    scores = []
    for kv in range(N_WIN_KV):
        q = jnp.concatenate([q_ref[:, head_cols(kv, g)] for g in range(WIN_GROUP)], axis=0)
        scores.append(lax.dot_general(q, band(kp_ref, kc_ref, kn_ref, kv), (((1,), (1,)), ((), ())),
                                      preferred_element_type=F32))
    for kv in range(N_WIN_KV):
        s = scores[kv] * (HEAD_DIM ** -0.5) + bias_ref[kv]
        s = jnp.where(mask, s, NEG)
        sink = jnp.concatenate([jnp.full((BLOCK, 1), sink_ref[kv * WIN_GROUP + g], F32)
                                for g in range(WIN_GROUP)], axis=0)
        m = jnp.maximum(jnp.max(s, axis=-1, keepdims=True), sink)
        e = jnp.exp(s - m)
        den = jnp.sum(e, axis=-1, keepdims=True) + jnp.exp(sink - m)
        o = jnp.dot(e.astype(BF16), band(vp_ref, vc_ref, vn_ref, kv), preferred_element_type=F32) / den
        for g in range(WIN_GROUP):
            o_ref[:, head_cols(kv, g)] = o[g * BLOCK:(g + 1) * BLOCK, :].astype(o_ref.dtype)


def _window_attention(proj, batch, seq, sink, bias_tiles):
    nblk = seq // BLOCK
    prev = lambda b, n: b * nblk + jnp.maximum(n - 1, 0)
    cur = lambda b, n: b * nblk + n
    nxt = lambda b, n: b * nblk + jnp.minimum(n + 1, nblk - 1)
    kspec = lambda rows, blk: pl.BlockSpec((BLOCK, WIN_KV_WIDTH),
                                           lambda b, n: (rows(b, n), blk // N_WIN_KV))
    kernel = functools.partial(_win_attn_kernel, seq=seq)
    return pl.pallas_call(
        kernel,
        grid=(batch, nblk),
        in_specs=[
            pl.BlockSpec(memory_space=pltpu.SMEM),
            pl.BlockSpec((BLOCK, WIN_WIDTH), lambda b, n: (b * nblk + n, QW_COL // WIN_WIDTH)),
            kspec(prev, KW_BLK), kspec(cur, KW_BLK), kspec(nxt, KW_BLK),
            kspec(prev, VW_BLK), kspec(cur, VW_BLK), kspec(nxt, VW_BLK),
            pl.BlockSpec((N_WIN_KV, WIN_GROUP * BLOCK, 3 * BLOCK), lambda b, n: (0, 0, 0)),
        ],
        out_specs=pl.BlockSpec((BLOCK, WIN_WIDTH), lambda b, n: (b * nblk + n, 0)),
        out_shape=jax.ShapeDtypeStruct((batch * seq, WIN_WIDTH), BF16),
        compiler_params=_params(("arbitrary", "arbitrary")),
        name="win_attn",
    )(sink.astype(F32), proj, proj, proj, proj, proj, proj, proj, bias_tiles)


def _out_proj_kernel(x_ref, oa_ref, ow_ref, wa_ref, ww_ref, o_ref):
    acc = jnp.dot(oa_ref[...], wa_ref[...], preferred_element_type=F32)
    acc = acc + jnp.dot(ow_ref[...], ww_ref[...], preferred_element_type=F32)
    o_ref[...] = x_ref[...] + acc


def _out_proj(x, oa, ow, wa, ww):
    n_tok, d = x.shape
    tm = min(ROW_TILE_OUT, n_tok)
    return pl.pallas_call(
        _out_proj_kernel,
        grid=(n_tok // tm,),
        in_specs=[
            pl.BlockSpec((tm, d), lambda i: (i, 0)),
            pl.BlockSpec((tm, DIFF_WIDTH), lambda i: (i, 0)),
            pl.BlockSpec((tm, WIN_WIDTH), lambda i: (i, 0)),
            pl.BlockSpec((DIFF_WIDTH, d), lambda i: (0, 0)),
            pl.BlockSpec((WIN_WIDTH, d), lambda i: (0, 0)),
        ],
        out_specs=pl.BlockSpec((tm, d), lambda i: (i, 0)),
        out_shape=jax.ShapeDtypeStruct((n_tok, d), F32),
        compiler_params=_params(("arbitrary",)),
        name="out_proj",
    )(x, oa, ow, wa, ww)


def _ffn_kernel(x_ref, g_ref, w1_ref, w2_ref, fg_ref, o_ref, h_ref, *, final_norm):
    f = pl.program_id(1)

    @pl.when(f == 0)
    def _():
        _norm_into(x_ref, g_ref, h_ref, copy_ref=o_ref)

    rows = h_ref.shape[0] // FF_ROW_GROUPS
    groups = [pl.ds(i * rows, rows) for i in range(FF_ROW_GROUPS)]
    hidden = [jnp.dot(h_ref[grp, :], w1_ref[...], preferred_element_type=F32) for grp in groups]
    for grp, a in zip(groups, hidden):
        a = jnp.square(jnp.maximum(a, 0.0)).astype(BF16)
        o_ref[grp, :] += jnp.dot(a, w2_ref[...], preferred_element_type=F32)

    if final_norm:
        @pl.when(f == pl.num_programs(1) - 1)
        def _():
            _norm_into(o_ref, fg_ref, o_ref)


def _ffn(x, g, w1, w2, final_g, final_norm):
    n_tok, d = x.shape
    d_ff = w1.shape[1]
    tm = min(ROW_TILE_FF, n_tok)
    kernel = functools.partial(_ffn_kernel, final_norm=final_norm)
    return pl.pallas_call(
        kernel,
        grid=(n_tok // tm, d_ff // FF_TILE),
        in_specs=[
            pl.BlockSpec((tm, d), lambda i, f: (i, 0)),
            pl.BlockSpec((1, d), lambda i, f: (0, 0)),
            pl.BlockSpec((d, FF_TILE), lambda i, f: (0, f)),
            pl.BlockSpec((FF_TILE, d), lambda i, f: (f, 0)),
            pl.BlockSpec((1, d), lambda i, f: (0, 0)),
        ],
        out_specs=pl.BlockSpec((tm, d), lambda i, f: (i, 0)),
        out_shape=jax.ShapeDtypeStruct((n_tok, d), F32),
        scratch_shapes=[pltpu.VMEM((tm, d), BF16)],
        compiler_params=_params(("arbitrary", "arbitrary")),
        name="ffn",
    )(x, g.reshape(1, d), w1, w2, final_g.reshape(1, d))


def _bucket_lookup(table, bucket):
    bucket = jnp.asarray(bucket.astype(np.int8))
    cols = table.T.astype(F32).reshape((table.shape[1],) + (1,) * bucket.ndim + (N_BUCKETS,))
    out = jnp.zeros((table.shape[1],) + bucket.shape, F32)
    for b in range(N_BUCKETS):
        out = jnp.where(bucket == b, cols[..., b], out)
    return out


def _diff_bias_tiles(table_a):
    t = ATT_TILE
    k = np.arange(t)[:, None]
    q = np.arange(t)[None, :]
    bucket = np.stack([_t5_bucket_np((o - 1) * t + k - q) for o in range(3)]
                      + [np.full((t, t), N_BUCKETS - 1), np.full((t, t), N_BUCKETS // 2 - 1)])
    assert bucket.shape[0] == N_BIAS_TILES
    return _bucket_lookup(table_a * LOG2E, bucket)


def _win_bias_tiles(table_b):
    rel = np.arange(3 * BLOCK)[None, :] - WINDOW - np.arange(BLOCK)[:, None]
    tiles = _bucket_lookup(table_b, _t5_bucket_np(rel))
    tiles = jnp.where(jnp.asarray(np.abs(rel) <= WINDOW), tiles, NEG)
    return tiles.reshape(N_WIN_KV, WIN_GROUP * BLOCK, 3 * BLOCK)


def _trunk(x3, diff_bias, win_bias, cfar, norm1_g, w_in, lambda_q1, lambda_k1, lambda_q2, lambda_k2,
           diff_subln_g, sink_logit, w_out, norm2_g, w_ff_in, w_ff_out, final_norm_g):
    batch, seq, d = x3.shape
    x = x3.reshape(batch * seq, d)
    for l in range(DEPTH):
        lambda_init = 0.8 - 0.6 * math.exp(-0.3 * l)
        proj = _in_proj(x, norm1_g[l], w_in[l])
        oa = _diff_attention(proj, batch, seq, cfar, diff_bias, lambda_q1[l], lambda_k1[l],
                             lambda_q2[l], lambda_k2[l], diff_subln_g[l], lambda_init)
        ow = _window_attention(proj, batch, seq, sink_logit[l], win_bias)
        x = _out_proj(x, oa, ow, w_out[l, :DIFF_WIDTH], w_out[l, DIFF_WIDTH:])
        x = _ffn(x, norm2_g[l], w_ff_in[l], w_ff_out[l], final_norm_g, final_norm=(l == DEPTH - 1))
    return x.reshape(batch, seq, d)


def kernel(x_prompt, x_sample, rel_bias, norm1_g, w_in, lambda_q1, lambda_k1, lambda_q2, lambda_k2,
           diff_subln_g, sink_logit, w_out, norm2_g, w_ff_in, w_ff_out, final_norm_g):
    table_a = rel_bias[:, :N_DIFF_HEADS]
    table_b = rel_bias[:, N_DIFF_HEADS:]
    diff_bias = _diff_bias_tiles(table_a)
    win_bias = _win_bias_tiles(table_b)
    cfar = jnp.stack([table_a[N_BUCKETS // 2 - 1], table_a[N_BUCKETS - 1]]).astype(F32) * LOG2E
    args = (diff_bias, win_bias, cfar, norm1_g, w_in.astype(BF16), lambda_q1, lambda_k1, lambda_q2,
            lambda_k2, diff_subln_g, sink_logit, w_out.astype(BF16), norm2_g, w_ff_in.astype(BF16),
            w_ff_out.astype(BF16), final_norm_g)
    return (_trunk(x_prompt, *args), _trunk(x_sample, *args))
```

```python
import functools
import math

import numpy as np
import jax
import jax.numpy as jnp
from jax import lax
from jax.experimental import pallas as pl
from jax.experimental.pallas import tpu as pltpu

D_MODEL = 2048
DEPTH = 2
HEAD_DIM = 128
N_DIFF_HEADS = 8
DIFF_QK_DIM = HEAD_DIM // 2
N_WIN_HEADS = 8
N_WIN_KV = 2
WIN_GROUP = N_WIN_HEADS // N_WIN_KV
WINDOW = 128
BLOCK = 128
N_BUCKETS = 32
MAX_DISTANCE = 128
D_FF = 4 * D_MODEL
EPS = 1e-6
NEG = -1e30
DIFF_WIDTH = N_DIFF_HEADS * HEAD_DIM
WIN_WIDTH = N_WIN_HEADS * HEAD_DIM
WIN_KV_WIDTH = N_WIN_KV * HEAD_DIM
IN_WIDTH = 3 * DIFF_WIDTH + WIN_WIDTH + 2 * WIN_KV_WIDTH

QA_BLK = 0
KA_BLK = DIFF_WIDTH // HEAD_DIM
VA_BLK = 2 * DIFF_WIDTH // HEAD_DIM
QW_COL = 3 * DIFF_WIDTH
KW_BLK = (3 * DIFF_WIDTH + WIN_WIDTH) // HEAD_DIM
VW_BLK = KW_BLK + N_WIN_KV

VMEM_LIMIT_BYTES = 56 * 1024 * 1024

LOG2E = math.log2(math.e)

ATT_TILE = 512
N_BIAS_TILES = 5
SUM_ROWS = 16
FAR_UNROLL = 2
DIFF_ATTN_FLAGS = None
ROW_TILE_IN = 1024
COL_TILE_IN = 512
ROW_TILE_OUT = 512
ROW_TILE_FF = 512
FF_TILE = 512
FF_ROW_GROUPS = 2
NORM_ROWS = 128

BF16 = jnp.bfloat16
F32 = jnp.float32


def _t5_bucket_np(rel):
    nb = N_BUCKETS // 2
    max_exact = nb // 2
    ret = np.where(rel > 0, nb, 0)
    n = np.abs(rel)
    nf = np.maximum(n, 1).astype(np.float32)
    large = max_exact + (np.log(nf / np.float32(max_exact)) / np.float32(math.log(MAX_DISTANCE / max_exact))
                         * np.float32(nb - max_exact)).astype(np.int32)
    large = np.minimum(large, nb - 1)
    return (ret + np.where(n < max_exact, n, large)).astype(np.int32)


_FAR = int(np.min(np.nonzero(_t5_bucket_np(-np.arange(0, 4 * MAX_DISTANCE)) == N_BUCKETS // 2 - 1)[0]))
assert np.all(_t5_bucket_np(-np.arange(_FAR, 1 << 16)) == N_BUCKETS // 2 - 1)
assert np.all(_t5_bucket_np(np.arange(_FAR, 1 << 16)) == N_BUCKETS - 1)
assert ATT_TILE >= _FAR


def _params(semantics, flags=None):
    return pltpu.CompilerParams(dimension_semantics=semantics, vmem_limit_bytes=VMEM_LIMIT_BYTES,
                                flags=flags)


def _rmsnorm_rows(x, g):
    ms = jnp.mean(x * x, axis=-1, keepdims=True)
    return x * lax.rsqrt(ms + EPS) * g


def _norm_into(x_ref, g_ref, h_ref, copy_ref=None):
    rows = x_ref.shape[0]

    def body(r, carry):
        r0 = pl.multiple_of(r * NORM_ROWS, NORM_ROWS)
        x = x_ref[pl.ds(r0, NORM_ROWS), :]
        h_ref[pl.ds(r0, NORM_ROWS), :] = _rmsnorm_rows(x, g_ref[...]).astype(h_ref.dtype)
        if copy_ref is not None:
            copy_ref[pl.ds(r0, NORM_ROWS), :] = x
        return carry

    lax.fori_loop(0, rows // NORM_ROWS, body, 0)


def _in_proj_kernel(x_ref, g_ref, w_ref, cs_ref, o_ref, h_ref):
    @pl.when(pl.program_id(1) == 0)
    def _():
        _norm_into(x_ref, g_ref, h_ref)

    acc = jnp.dot(h_ref[...], w_ref[...], preferred_element_type=F32)
    o_ref[...] = (acc * cs_ref[...]).astype(o_ref.dtype)


def _in_proj_col_scale():
    cs = np.ones((1, IN_WIDTH), np.float32)
    cs[:, :DIFF_WIDTH] = LOG2E * DIFF_QK_DIM ** -0.5
    return jnp.asarray(cs)


def _in_proj(x, g, w):
    n_tok, d = x.shape
    n_out = w.shape[1]
    tm = min(ROW_TILE_IN, n_tok)
    return pl.pallas_call(
        _in_proj_kernel,
        grid=(n_tok // tm, n_out // COL_TILE_IN),
        in_specs=[
            pl.BlockSpec((tm, d), lambda i, j: (i, 0)),
            pl.BlockSpec((1, d), lambda i, j: (0, 0)),
            pl.BlockSpec((d, COL_TILE_IN), lambda i, j: (0, j)),
            pl.BlockSpec((1, COL_TILE_IN), lambda i, j: (0, j)),
        ],
        out_specs=pl.BlockSpec((tm, COL_TILE_IN), lambda i, j: (i, j)),
        out_shape=jax.ShapeDtypeStruct((n_tok, n_out), BF16),
        scratch_shapes=[pltpu.VMEM((tm, d), BF16)],
        compiler_params=_params(("arbitrary", "arbitrary")),
        name="in_proj",
    )(x, g.reshape(1, d), w, _in_proj_col_scale())


def _diff_attn_kernel(cfar_ref, zero_ref, q_ref, k_ref, v_ref, bias_ref, lq1_ref, lk1_ref, lq2_ref, lk2_ref,
                      g_ref, o_ref, vt_ref, qt_ref, s00_ref, s01_ref, s10_ref, s11_ref,
                      p00_ref, p01_ref, p10_ref, p11_ref, smax_ref,
                      m_ref, alpha_ref, acc_ref, *, n_chunks, lambda_init):
    t = ATT_TILE
    h = pl.program_id(1)
    qi = pl.program_id(2)
    s_refs = ((s00_ref, s01_ref), (s10_ref, s11_ref))
    p_refs = ((p00_ref, p01_ref), (p10_ref, p11_ref))
    half_cols = (slice(0, t), slice(t, 2 * t))

    @pl.when(qi == 0)
    def _():
        def body(j, carry):
            r0 = pl.multiple_of(j * t, t)
            vt_ref[j, :HEAD_DIM] = v_ref[pl.ds(r0, t), :].astype(F32).T.astype(BF16)
            vt_ref[j, HEAD_DIM:] = jnp.ones((SUM_ROWS, t), BF16)
            return carry

        lax.fori_loop(0, n_chunks, body, 0)

    qt = q_ref[...].astype(F32).T
    row = lax.broadcasted_iota(jnp.int32, qt.shape, 0)
    qt_ref[:, :t] = jnp.where(row < DIFF_QK_DIM, qt, 0.0).astype(BF16)
    qt_ref[:, t:] = jnp.where(row >= DIFF_QK_DIM, qt, 0.0).astype(BF16)
    m_ref[...] = jnp.full(m_ref.shape, NEG, F32)
    acc_ref[...] = jnp.zeros(acc_ref.shape, F32)
    alpha_ref[...] = jnp.ones(alpha_ref.shape, F32)
    p11_ref[...] = jnp.zeros(p11_ref.shape, BF16)

    def chunk_of(r):
        return lax.rem(qi + r + (n_chunks - 1), n_chunks)

    def qk(j, half, par, tile):
        cols = half_cols[half]
        r0 = pl.multiple_of(j * t, t)
        s = jnp.dot(k_ref[pl.ds(r0, t), :], qt_ref[:, cols], preferred_element_type=F32)
        if tile is not None:
            s = s + bias_ref[tile]
        s_refs[half][par][...] = s
        smax_ref[:, cols] = jnp.max(s, axis=0, keepdims=True)

    def softmax(half, par, c):
        cols = half_cols[half]
        m_old = m_ref[:, cols]
        m_new = jnp.maximum(m_old, smax_ref[:, cols] + c)
        alpha = jnp.exp2(m_old - m_new)
        s = s_refs[half][par][pl.ds(pl.multiple_of(zero_ref[0], t), t), :]
        p_refs[half][par][...] = jnp.exp2(s - (m_new - c)).astype(BF16)
        alpha_ref[:, cols] = alpha
        m_ref[:, cols] = m_new

    def pv(j, half, par):
        cols = half_cols[half]
        acc_ref[:, cols] = alpha_ref[:, cols] * acc_ref[:, cols] + jnp.dot(
            vt_ref[j], p_refs[half][par][...], preferred_element_type=F32)

    def step(r, par, tile, next_tile, c):
        j = chunk_of(r)
        qk(j, 1, par, tile)
        pv(chunk_of(r - 1), 1, 1 - par)
        softmax(0, par, c)
        pv(j, 0, par)
        qk(chunk_of(r + 1), 0, 1 - par, next_tile)
        softmax(1, par, c)

    c_lo = cfar_ref[0, h]
    c_hi = cfar_ref[1, h]

    def far_step(r, par):
        step(r, par, None, None, jnp.where(chunk_of(r) < qi, c_lo, c_hi))

    def far_group(i, carry):
        for u in range(FAR_UNROLL):
            far_step(4 + FAR_UNROLL * i + u, u % 2)
        return carry

    near_tiles = (jnp.where(qi >= 1, 0, 3), 1, jnp.where(qi <= n_chunks - 2, 2, 4))
    qk(chunk_of(0), 0, 0, near_tiles[0])
    step(0, 0, near_tiles[0], near_tiles[1], 0.0)
    step(1, 1, near_tiles[1], near_tiles[2], 0.0)
    step(2, 0, near_tiles[2], None, 0.0)
    far_step(3, 1)
    lax.fori_loop(0, (n_chunks - 4) // FAR_UNROLL, far_group, 0)
    pv(chunk_of(n_chunks - 1), 1, (n_chunks - 1) % 2)

    inv_l = 1.0 / acc_ref[HEAD_DIM:HEAD_DIM + 1, :]
    acc = acc_ref[:HEAD_DIM, :]
    o0 = acc[:, :t] * inv_l[:, :t]
    o1 = acc[:, t:] * inv_l[:, t:]
    lam = (jnp.exp(jnp.sum(lq1_ref[...] * lk1_ref[...], axis=1, keepdims=True))
           - jnp.exp(jnp.sum(lq2_ref[...] * lk2_ref[...], axis=1, keepdims=True)) + lambda_init)
    ot = o0 - lam * o1
    ms = jnp.mean(ot * ot, axis=0, keepdims=True)
    y = ot * lax.rsqrt(ms + EPS) * g_ref[...] * (1.0 - lambda_init)
    o_ref[...] = y.T.astype(o_ref.dtype)


def _diff_attention(proj, batch, seq, cfar, bias_tiles, lq1, lk1, lq2, lk2, g, lambda_init):
    t = ATT_TILE
    nq = seq // t
    assert seq % t == 0 and nq >= 4 and (nq - 4) % FAR_UNROLL == 0, "4 leading chunks, then groups"
    vec = lambda a: a.reshape(1, DIFF_QK_DIM).astype(F32)
    vec_spec = pl.BlockSpec((1, DIFF_QK_DIM), lambda b, h, qi: (0, 0))
    kernel = functools.partial(_diff_attn_kernel, n_chunks=nq, lambda_init=lambda_init)
    return pl.pallas_call(
        kernel,
        grid=(batch, N_DIFF_HEADS, nq),
        in_specs=[
            pl.BlockSpec(memory_space=pltpu.SMEM),
            pl.BlockSpec(memory_space=pltpu.SMEM),
            pl.BlockSpec((t, HEAD_DIM), lambda b, h, qi: (b * nq + qi, QA_BLK + h)),
            pl.BlockSpec((seq, HEAD_DIM), lambda b, h, qi: (b, KA_BLK + h)),
            pl.BlockSpec((seq, HEAD_DIM), lambda b, h, qi: (b, VA_BLK + h)),
            pl.BlockSpec((None, N_BIAS_TILES, t, t), lambda b, h, qi: (h, 0, 0, 0)),
            vec_spec, vec_spec, vec_spec, vec_spec,
            pl.BlockSpec((HEAD_DIM, 1), lambda b, h, qi: (0, 0)),
        ],
        out_specs=pl.BlockSpec((t, HEAD_DIM), lambda b, h, qi: (b * nq + qi, h)),
        out_shape=jax.ShapeDtypeStruct((batch * seq, DIFF_WIDTH), BF16),
        scratch_shapes=[
            pltpu.VMEM((nq, HEAD_DIM + SUM_ROWS, t), BF16),
            pltpu.VMEM((HEAD_DIM, 2 * t), BF16),
            pltpu.VMEM((t, t), F32),
            pltpu.VMEM((t, t), F32),
            pltpu.VMEM((t, t), F32),
            pltpu.VMEM((t, t), F32),
            pltpu.VMEM((t, t), BF16),
            pltpu.VMEM((t, t), BF16),
            pltpu.VMEM((t, t), BF16),
            pltpu.VMEM((t, t), BF16),
            pltpu.VMEM((1, 2 * t), F32),
            pltpu.VMEM((1, 2 * t), F32),
            pltpu.VMEM((1, 2 * t), F32),
            pltpu.VMEM((HEAD_DIM + SUM_ROWS, 2 * t), F32),
        ],
        compiler_params=_params(("arbitrary", "arbitrary", "arbitrary"), DIFF_ATTN_FLAGS),
        name="diff_attn",
    )(cfar, jnp.zeros((1,), jnp.int32), proj, proj, proj, bias_tiles, vec(lq1), vec(lk1), vec(lq2), vec(lk2),
      g.reshape(HEAD_DIM, 1).astype(F32))


def _win_attn_kernel(sink_ref, q_ref, kp_ref, kc_ref, kn_ref, vp_ref, vc_ref, vn_ref, bias_ref, o_ref,
                     *, seq):
    n = pl.program_id(1)
    key_pos = n * BLOCK - WINDOW + lax.broadcasted_iota(jnp.int32, (1, 3 * BLOCK), 1)
    mask = (key_pos >= 0) & (key_pos < seq)

    def band(prev_ref, cur_ref, next_ref, kv):
        cols = slice(kv * HEAD_DIM, (kv + 1) * HEAD_DIM)
        return jnp.concatenate([prev_ref[:, cols], cur_ref[:, cols], next_ref[:, cols]], axis=0)

    def head_cols(kv, g):
        h = kv * WIN_GROUP + g
        return slice(h * HEAD_DIM, (h + 1) * HEAD_DIM)

    scores = []
    for kv in range(N_WIN_KV):
        q = jnp.concatenate([q_ref[:, head_cols(kv, g)] for g in range(WIN_GROUP)], axis=0)
        scores.append(lax.dot_general(q, band(kp_ref, kc_ref, kn_ref, kv), (((1,), (1,)), ((), ())),
                                      preferred_element_type=F32))
    for kv in range(N_WIN_KV):
        s = scores[kv] * (HEAD_DIM ** -0.5) + bias_ref[kv]
        s = jnp.where(mask, s, NEG)
        sink = jnp.concatenate([jnp.full((BLOCK, 1), sink_ref[kv * WIN_GROUP + g], F32)
                                for g in range(WIN_GROUP)], axis=0)
        m = jnp.maximum(jnp.max(s, axis=-1, keepdims=True), sink)
        e = jnp.exp(s - m)
        den = jnp.sum(e, axis=-1, keepdims=True) + jnp.exp(sink - m)
        o = jnp.dot(e.astype(BF16), band(vp_ref, vc_ref, vn_ref, kv), preferred_element_type=F32) / den
        for g in range(WIN_GROUP):
            o_ref[:, head_cols(kv, g)] = o[g * BLOCK:(g + 1) * BLOCK, :].astype(o_ref.dtype)


def _window_attention(proj, batch, seq, sink, bias_tiles):
    nblk = seq // BLOCK
    prev = lambda b, n: b * nblk + jnp.maximum(n - 1, 0)
    cur = lambda b, n: b * nblk + n
    nxt = lambda b, n: b * nblk + jnp.minimum(n + 1, nblk - 1)
    kspec = lambda rows, blk: pl.BlockSpec((BLOCK, WIN_KV_WIDTH),
                                           lambda b, n: (rows(b, n), blk // N_WIN_KV))
    kernel = functools.partial(_win_attn_kernel, seq=seq)
    return pl.pallas_call(
        kernel,
        grid=(batch, nblk),
        in_specs=[
            pl.BlockSpec(memory_space=pltpu.SMEM),
            pl.BlockSpec((BLOCK, WIN_WIDTH), lambda b, n: (b * nblk + n, QW_COL // WIN_WIDTH)),
            kspec(prev, KW_BLK), kspec(cur, KW_BLK), kspec(nxt, KW_BLK),
            kspec(prev, VW_BLK), kspec(cur, VW_BLK), kspec(nxt, VW_BLK),
            pl.BlockSpec((N_WIN_KV, WIN_GROUP * BLOCK, 3 * BLOCK), lambda b, n: (0, 0, 0)),
        ],
        out_specs=pl.BlockSpec((BLOCK, WIN_WIDTH), lambda b, n: (b * nblk + n, 0)),
        out_shape=jax.ShapeDtypeStruct((batch * seq, WIN_WIDTH), BF16),
        compiler_params=_params(("arbitrary", "arbitrary")),
        name="win_attn",
    )(sink.astype(F32), proj, proj, proj, proj, proj, proj, proj, bias_tiles)


def _out_proj_kernel(x_ref, oa_ref, ow_ref, wa_ref, ww_ref, o_ref):
    acc = jnp.dot(oa_ref[...], wa_ref[...], preferred_element_type=F32)
    acc = acc + jnp.dot(ow_ref[...], ww_ref[...], preferred_element_type=F32)
    o_ref[...] = x_ref[...] + acc


def _out_proj(x, oa, ow, wa, ww):
    n_tok, d = x.shape
    tm = min(ROW_TILE_OUT, n_tok)
    return pl.pallas_call(
        _out_proj_kernel,
        grid=(n_tok // tm,),
        in_specs=[
            pl.BlockSpec((tm, d), lambda i: (i, 0)),
            pl.BlockSpec((tm, DIFF_WIDTH), lambda i: (i, 0)),
            pl.BlockSpec((tm, WIN_WIDTH), lambda i: (i, 0)),
            pl.BlockSpec((DIFF_WIDTH, d), lambda i: (0, 0)),
            pl.BlockSpec((WIN_WIDTH, d), lambda i: (0, 0)),
        ],
        out_specs=pl.BlockSpec((tm, d), lambda i: (i, 0)),
        out_shape=jax.ShapeDtypeStruct((n_tok, d), F32),
        compiler_params=_params(("arbitrary",)),
        name="out_proj",
    )(x, oa, ow, wa, ww)


def _ffn_kernel(x_ref, g_ref, w1_ref, w2_ref, fg_ref, o_ref, h_ref, *, final_norm):
    f = pl.program_id(1)

    @pl.when(f == 0)
    def _():
        _norm_into(x_ref, g_ref, h_ref, copy_ref=o_ref)

    rows = h_ref.shape[0] // FF_ROW_GROUPS
    groups = [pl.ds(i * rows, rows) for i in range(FF_ROW_GROUPS)]
    hidden = [jnp.dot(h_ref[grp, :], w1_ref[...], preferred_element_type=F32) for grp in groups]
    for grp, a in zip(groups, hidden):
        a = jnp.square(jnp.maximum(a, 0.0)).astype(BF16)
        o_ref[grp, :] += jnp.dot(a, w2_ref[...], preferred_element_type=F32)

    if final_norm:
        @pl.when(f == pl.num_programs(1) - 1)
        def _():
            _norm_into(o_ref, fg_ref, o_ref)


def _ffn(x, g, w1, w2, final_g, final_norm):
    n_tok, d = x.shape
    d_ff = w1.shape[1]
    tm = min(ROW_TILE_FF, n_tok)
    kernel = functools.partial(_ffn_kernel, final_norm=final_norm)
    return pl.pallas_call(
        kernel,
        grid=(n_tok // tm, d_ff // FF_TILE),
        in_specs=[
            pl.BlockSpec((tm, d), lambda i, f: (i, 0)),
            pl.BlockSpec((1, d), lambda i, f: (0, 0)),
            pl.BlockSpec((d, FF_TILE), lambda i, f: (0, f)),
            pl.BlockSpec((FF_TILE, d), lambda i, f: (f, 0)),
            pl.BlockSpec((1, d), lambda i, f: (0, 0)),
        ],
        out_specs=pl.BlockSpec((tm, d), lambda i, f: (i, 0)),
        out_shape=jax.ShapeDtypeStruct((n_tok, d), F32),
        scratch_shapes=[pltpu.VMEM((tm, d), BF16)],
        compiler_params=_params(("arbitrary", "arbitrary")),
        name="ffn",
    )(x, g.reshape(1, d), w1, w2, final_g.reshape(1, d))


def _bucket_lookup(table, bucket):
    bucket = jnp.asarray(bucket.astype(np.int8))
    cols = table.T.astype(F32).reshape((table.shape[1],) + (1,) * bucket.ndim + (N_BUCKETS,))
    out = jnp.zeros((table.shape[1],) + bucket.shape, F32)
    for b in range(N_BUCKETS):
        out = jnp.where(bucket == b, cols[..., b], out)
    return out


def _diff_bias_tiles(table_a):
    t = ATT_TILE
    k = np.arange(t)[:, None]
    q = np.arange(t)[None, :]
    bucket = np.stack([_t5_bucket_np((o - 1) * t + k - q) for o in range(3)]
                      + [np.full((t, t), N_BUCKETS - 1), np.full((t, t), N_BUCKETS // 2 - 1)])
    assert bucket.shape[0] == N_BIAS_TILES
    return _bucket_lookup(table_a * LOG2E, bucket)


def _win_bias_tiles(table_b):
    rel = np.arange(3 * BLOCK)[None, :] - WINDOW - np.arange(BLOCK)[:, None]
    tiles = _bucket_lookup(table_b, _t5_bucket_np(rel))
    tiles = jnp.where(jnp.asarray(np.abs(rel) <= WINDOW), tiles, NEG)
    return tiles.reshape(N_WIN_KV, WIN_GROUP * BLOCK, 3 * BLOCK)


def _trunk(x3, diff_bias, win_bias, cfar, norm1_g, w_in, lambda_q1, lambda_k1, lambda_q2, lambda_k2,
           diff_subln_g, sink_logit, w_out, norm2_g, w_ff_in, w_ff_out, final_norm_g):
    batch, seq, d = x3.shape
    x = x3.reshape(batch * seq, d)
    for l in range(DEPTH):
        lambda_init = 0.8 - 0.6 * math.exp(-0.3 * l)
        proj = _in_proj(x, norm1_g[l], w_in[l])
        oa = _diff_attention(proj, batch, seq, cfar, diff_bias, lambda_q1[l], lambda_k1[l],
                             lambda_q2[l], lambda_k2[l], diff_subln_g[l], lambda_init)
        ow = _window_attention(proj, batch, seq, sink_logit[l], win_bias)
        x = _out_proj(x, oa, ow, w_out[l, :DIFF_WIDTH], w_out[l, DIFF_WIDTH:])
        x = _ffn(x, norm2_g[l], w_ff_in[l], w_ff_out[l], final_norm_g, final_norm=(l == DEPTH - 1))
    return x.reshape(batch, seq, d)


def kernel(x_prompt, x_sample, rel_bias, norm1_g, w_in, lambda_q1, lambda_k1, lambda_q2, lambda_k2,
           diff_subln_g, sink_logit, w_out, norm2_g, w_ff_in, w_ff_out, final_norm_g):
    table_a = rel_bias[:, :N_DIFF_HEADS]
    table_b = rel_bias[:, N_DIFF_HEADS:]
    diff_bias = _diff_bias_tiles(table_a)
    win_bias = _win_bias_tiles(table_b)
    cfar = jnp.stack([table_a[N_BUCKETS // 2 - 1], table_a[N_BUCKETS - 1]]).astype(F32) * LOG2E
    args = (diff_bias, win_bias, cfar, norm1_g, w_in.astype(BF16), lambda_q1, lambda_k1, lambda_q2,
            lambda_k2, diff_subln_g, sink_logit, w_out.astype(BF16), norm2_g, w_ff_in.astype(BF16),
            w_ff_out.astype(BF16), final_norm_g)
    return (_trunk(x_prompt, *args), _trunk(x_sample, *args))
```

```python
import functools
import math

import numpy as np
import jax
import jax.numpy as jnp
from jax import lax
from jax.experimental import pallas as pl
from jax.experimental.pallas import tpu as pltpu

D_MODEL = 2048
DEPTH = 2
HEAD_DIM = 128
N_DIFF_HEADS = 8
DIFF_QK_DIM = HEAD_DIM // 2
N_WIN_HEADS = 8
N_WIN_KV = 2
WIN_GROUP = N_WIN_HEADS // N_WIN_KV
WINDOW = 128
BLOCK = 128
N_BUCKETS = 32
MAX_DISTANCE = 128
D_FF = 4 * D_MODEL
EPS = 1e-6
NEG = -1e30
DIFF_WIDTH = N_DIFF_HEADS * HEAD_DIM
WIN_WIDTH = N_WIN_HEADS * HEAD_DIM
WIN_KV_WIDTH = N_WIN_KV * HEAD_DIM
IN_WIDTH = 3 * DIFF_WIDTH + WIN_WIDTH + 2 * WIN_KV_WIDTH

QA_BLK = 0
KA_BLK = DIFF_WIDTH // HEAD_DIM
VA_BLK = 2 * DIFF_WIDTH // HEAD_DIM
QW_COL = 3 * DIFF_WIDTH
KW_BLK = (3 * DIFF_WIDTH + WIN_WIDTH) // HEAD_DIM
VW_BLK = KW_BLK + N_WIN_KV

VMEM_LIMIT_BYTES = 56 * 1024 * 1024

LOG2E = math.log2(math.e)

ATT_TILE = 512
N_BIAS_TILES = 5
SUM_ROWS = 16
FAR_UNROLL = 2
DIFF_ATTN_FLAGS = None
ROW_TILE_IN = 1024
COL_TILE_IN = 512
ROW_TILE_OUT = 512
ROW_TILE_FF = 1024
FF_TILE = 512
FF_ROW_GROUPS = 2
NORM_ROWS = 128
WIN_Q_BLOCKS = 4

BF16 = jnp.bfloat16
F32 = jnp.float32


def _t5_bucket_np(rel):
    nb = N_BUCKETS // 2
    max_exact = nb // 2
    ret = np.where(rel > 0, nb, 0)
    n = np.abs(rel)
    nf = np.maximum(n, 1).astype(np.float32)
    large = max_exact + (np.log(nf / np.float32(max_exact)) / np.float32(math.log(MAX_DISTANCE / max_exact))
                         * np.float32(nb - max_exact)).astype(np.int32)
    large = np.minimum(large, nb - 1)
    return (ret + np.where(n < max_exact, n, large)).astype(np.int32)


_FAR = int(np.min(np.nonzero(_t5_bucket_np(-np.arange(0, 4 * MAX_DISTANCE)) == N_BUCKETS // 2 - 1)[0]))
assert np.all(_t5_bucket_np(-np.arange(_FAR, 1 << 16)) == N_BUCKETS // 2 - 1)
assert np.all(_t5_bucket_np(np.arange(_FAR, 1 << 16)) == N_BUCKETS - 1)
assert ATT_TILE >= _FAR


def _params(semantics, flags=None):
    return pltpu.CompilerParams(dimension_semantics=semantics, vmem_limit_bytes=VMEM_LIMIT_BYTES,
                                flags=flags)


def _rmsnorm_rows(x, g):
    ms = jnp.mean(x * x, axis=-1, keepdims=True)
    return x * lax.rsqrt(ms + EPS) * g


def _norm_into(x_ref, g_ref, h_ref, copy_ref=None):
    rows = x_ref.shape[0]

    def body(r, carry):
        r0 = pl.multiple_of(r * NORM_ROWS, NORM_ROWS)
        x = x_ref[pl.ds(r0, NORM_ROWS), :]
        h_ref[pl.ds(r0, NORM_ROWS), :] = _rmsnorm_rows(x, g_ref[...]).astype(h_ref.dtype)
        if copy_ref is not None:
            copy_ref[pl.ds(r0, NORM_ROWS), :] = x
        return carry

    lax.fori_loop(0, rows // NORM_ROWS, body, 0)


def _in_proj_kernel(x_ref, g_ref, w_ref, cs_ref, o_ref, h_ref):
    @pl.when(pl.program_id(1) == 0)
    def _():
        _norm_into(x_ref, g_ref, h_ref)

    acc = jnp.dot(h_ref[...], w_ref[...], preferred_element_type=F32)
    o_ref[...] = (acc * cs_ref[...]).astype(o_ref.dtype)


def _in_proj_col_scale():
    cs = np.ones((1, IN_WIDTH), np.float32)
    cs[:, :DIFF_WIDTH] = LOG2E * DIFF_QK_DIM ** -0.5
    cs[:, QW_COL:QW_COL + WIN_WIDTH] = LOG2E * HEAD_DIM ** -0.5
    return jnp.asarray(cs)


def _in_proj(x, g, w):
    n_tok, d = x.shape
    n_out = w.shape[1]
    tm = min(ROW_TILE_IN, n_tok)
    return pl.pallas_call(
        _in_proj_kernel,
        grid=(n_tok // tm, n_out // COL_TILE_IN),
        in_specs=[
            pl.BlockSpec((tm, d), lambda i, j: (i, 0)),
            pl.BlockSpec((1, d), lambda i, j: (0, 0)),
            pl.BlockSpec((d, COL_TILE_IN), lambda i, j: (0, j)),
            pl.BlockSpec((1, COL_TILE_IN), lambda i, j: (0, j)),
        ],
        out_specs=pl.BlockSpec((tm, COL_TILE_IN), lambda i, j: (i, j)),
        out_shape=jax.ShapeDtypeStruct((n_tok, n_out), BF16),
        scratch_shapes=[pltpu.VMEM((tm, d), BF16)],
        compiler_params=_params(("arbitrary", "arbitrary")),
        name="in_proj",
    )(x, g.reshape(1, d), w, _in_proj_col_scale())


def _diff_attn_kernel(cfar_ref, zero_ref, q_ref, k_ref, v_ref, bias_ref, lq1_ref, lk1_ref, lq2_ref, lk2_ref,
                      g_ref, o_ref, vt_ref, qt_ref, s00_ref, s01_ref, s10_ref, s11_ref,
                      p00_ref, p01_ref, p10_ref, p11_ref, smax_ref,
                      m_ref, alpha_ref, acc_ref, *, n_chunks, lambda_init):
    t = ATT_TILE
    h = pl.program_id(1)
    qi = pl.program_id(2)
    s_refs = ((s00_ref, s01_ref), (s10_ref, s11_ref))
    p_refs = ((p00_ref, p01_ref), (p10_ref, p11_ref))
    half_cols = (slice(0, t), slice(t, 2 * t))

    @pl.when(qi == 0)
    def _():
        def body(j, carry):
            r0 = pl.multiple_of(j * t, t)
            vt_ref[j, :HEAD_DIM] = v_ref[pl.ds(r0, t), :].astype(F32).T.astype(BF16)
            vt_ref[j, HEAD_DIM:] = jnp.ones((SUM_ROWS, t), BF16)
            return carry

        lax.fori_loop(0, n_chunks, body, 0)

    qt = q_ref[...].astype(F32).T
    row = lax.broadcasted_iota(jnp.int32, qt.shape, 0)
    qt_ref[:, :t] = jnp.where(row < DIFF_QK_DIM, qt, 0.0).astype(BF16)
    qt_ref[:, t:] = jnp.where(row >= DIFF_QK_DIM, qt, 0.0).astype(BF16)
    m_ref[...] = jnp.full(m_ref.shape, NEG, F32)
    acc_ref[...] = jnp.zeros(acc_ref.shape, F32)
    alpha_ref[...] = jnp.ones(alpha_ref.shape, F32)
    p11_ref[...] = jnp.zeros(p11_ref.shape, BF16)

    def chunk_of(r):
        return lax.rem(qi + r + (n_chunks - 1), n_chunks)

    def qk(j, half, par, tile):
        cols = half_cols[half]
        r0 = pl.multiple_of(j * t, t)
        s = jnp.dot(k_ref[pl.ds(r0, t), :], qt_ref[:, cols], preferred_element_type=F32)
        if tile is not None:
            s = s + bias_ref[tile]
        s_refs[half][par][...] = s
        smax_ref[:, cols] = jnp.max(s, axis=0, keepdims=True)

    def softmax(half, par, c):
        cols = half_cols[half]
        m_old = m_ref[:, cols]
        m_new = jnp.maximum(m_old, smax_ref[:, cols] + c)
        alpha = jnp.exp2(m_old - m_new)
        s = s_refs[half][par][pl.ds(pl.multiple_of(zero_ref[0], t), t), :]
        p_refs[half][par][...] = jnp.exp2(s - (m_new - c)).astype(BF16)
        alpha_ref[:, cols] = alpha
        m_ref[:, cols] = m_new

    def pv(j, half, par):
        cols = half_cols[half]
        acc_ref[:, cols] = alpha_ref[:, cols] * acc_ref[:, cols] + jnp.dot(
            vt_ref[j], p_refs[half][par][...], preferred_element_type=F32)

    def step(r, par, tile, next_tile, c):
        j = chunk_of(r)
        qk(j, 1, par, tile)
        pv(chunk_of(r - 1), 1, 1 - par)
        softmax(0, par, c)
        pv(j, 0, par)
        qk(chunk_of(r + 1), 0, 1 - par, next_tile)
        softmax(1, par, c)

    c_lo = cfar_ref[0, h]
    c_hi = cfar_ref[1, h]

    def far_step(r, par):
        step(r, par, None, None, jnp.where(chunk_of(r) < qi, c_lo, c_hi))

    def far_group(i, carry):
        for u in range(FAR_UNROLL):
            far_step(4 + FAR_UNROLL * i + u, u % 2)
        return carry

    near_tiles = (jnp.where(qi >= 1, 0, 3), 1, jnp.where(qi <= n_chunks - 2, 2, 4))
    qk(chunk_of(0), 0, 0, near_tiles[0])
    step(0, 0, near_tiles[0], near_tiles[1], 0.0)
    step(1, 1, near_tiles[1], near_tiles[2], 0.0)
    step(2, 0, near_tiles[2], None, 0.0)
    far_step(3, 1)
    lax.fori_loop(0, (n_chunks - 4) // FAR_UNROLL, far_group, 0)
    pv(chunk_of(n_chunks - 1), 1, (n_chunks - 1) % 2)

    inv_l = 1.0 / acc_ref[HEAD_DIM:HEAD_DIM + 1, :]
    acc = acc_ref[:HEAD_DIM, :]
    o0 = acc[:, :t] * inv_l[:, :t]
    o1 = acc[:, t:] * inv_l[:, t:]
    lam = (jnp.exp(jnp.sum(lq1_ref[...] * lk1_ref[...], axis=1, keepdims=True))
           - jnp.exp(jnp.sum(lq2_ref[...] * lk2_ref[...], axis=1, keepdims=True)) + lambda_init)
    ot = o0 - lam * o1
    ms = jnp.mean(ot * ot, axis=0, keepdims=True)
    y = ot * lax.rsqrt(ms + EPS) * g_ref[...] * (1.0 - lambda_init)
    o_ref[...] = y.T.astype(o_ref.dtype)


def _diff_attention(proj, batch, seq, cfar, bias_tiles, lq1, lk1, lq2, lk2, g, lambda_init):
    t = ATT_TILE
    nq = seq // t
    assert seq % t == 0 and nq >= 4 and (nq - 4) % FAR_UNROLL == 0, "4 leading chunks, then groups"
    vec = lambda a: a.reshape(1, DIFF_QK_DIM).astype(F32)
    vec_spec = pl.BlockSpec((1, DIFF_QK_DIM), lambda b, h, qi: (0, 0))
    kernel = functools.partial(_diff_attn_kernel, n_chunks=nq, lambda_init=lambda_init)
    return pl.pallas_call(
        kernel,
        grid=(batch, N_DIFF_HEADS, nq),
        in_specs=[
            pl.BlockSpec(memory_space=pltpu.SMEM),
            pl.BlockSpec(memory_space=pltpu.SMEM),
            pl.BlockSpec((t, HEAD_DIM), lambda b, h, qi: (b * nq + qi, QA_BLK + h)),
            pl.BlockSpec((seq, HEAD_DIM), lambda b, h, qi: (b, KA_BLK + h)),
            pl.BlockSpec((seq, HEAD_DIM), lambda b, h, qi: (b, VA_BLK + h)),
            pl.BlockSpec((None, N_BIAS_TILES, t, t), lambda b, h, qi: (h, 0, 0, 0)),
            vec_spec, vec_spec, vec_spec, vec_spec,
            pl.BlockSpec((HEAD_DIM, 1), lambda b, h, qi: (0, 0)),
        ],
        out_specs=pl.BlockSpec((t, HEAD_DIM), lambda b, h, qi: (b * nq + qi, h)),
        out_shape=jax.ShapeDtypeStruct((batch * seq, DIFF_WIDTH), BF16),
        scratch_shapes=[
            pltpu.VMEM((nq, HEAD_DIM + SUM_ROWS, t), BF16),
            pltpu.VMEM((HEAD_DIM, 2 * t), BF16),
            pltpu.VMEM((t, t), F32),
            pltpu.VMEM((t, t), F32),
            pltpu.VMEM((t, t), F32),
            pltpu.VMEM((t, t), F32),
            pltpu.VMEM((t, t), BF16),
            pltpu.VMEM((t, t), BF16),
            pltpu.VMEM((t, t), BF16),
            pltpu.VMEM((t, t), BF16),
            pltpu.VMEM((1, 2 * t), F32),
            pltpu.VMEM((1, 2 * t), F32),
            pltpu.VMEM((1, 2 * t), F32),
            pltpu.VMEM((HEAD_DIM + SUM_ROWS, 2 * t), F32),
        ],
        compiler_params=_params(("arbitrary", "arbitrary", "arbitrary"), DIFF_ATTN_FLAGS),
        name="diff_attn",
    )(cfar, jnp.zeros((1,), jnp.int32), proj, proj, proj, bias_tiles, vec(lq1), vec(lk1), vec(lq2), vec(lk2),
      g.reshape(HEAD_DIM, 1).astype(F32))


def _win_attn_kernel(sink_ref, q_ref, *refs, seq):
    n_band = WIN_Q_BLOCKS + 2
    k_refs, v_refs = refs[:n_band], refs[n_band:2 * n_band]
    bias_ref, o_ref = refs[2 * n_band], refs[2 * n_band + 1]
    first = pl.program_id(1) * WIN_Q_BLOCKS

    def band(block_refs, u, kv):
        cols = slice(kv * HEAD_DIM, (kv + 1) * HEAD_DIM)
        return jnp.concatenate([r[:, cols] for r in block_refs[u:u + 3]], axis=0)

    def head_cols(kv, g):
        h = kv * WIN_GROUP + g
        return slice(h * HEAD_DIM, (h + 1) * HEAD_DIM)

    chains = [(u, kv) for u in range(WIN_Q_BLOCKS) for kv in range(N_WIN_KV)]
    scores = []
    for u, kv in chains:
        rows = slice(u * BLOCK, (u + 1) * BLOCK)
        q = jnp.concatenate([q_ref[rows, head_cols(kv, g)] for g in range(WIN_GROUP)], axis=0)
        scores.append(lax.dot_general(q, band(k_refs, u, kv), (((1,), (1,)), ((), ())),
                                      preferred_element_type=F32))
    for (u, kv), s in zip(chains, scores):
        key_pos = (first + u) * BLOCK - WINDOW + lax.broadcasted_iota(jnp.int32, (1, 3 * BLOCK), 1)
        s = jnp.where((key_pos >= 0) & (key_pos < seq), s, NEG) + bias_ref[kv]
        sink = jnp.concatenate([jnp.full((BLOCK, 1), sink_ref[kv * WIN_GROUP + g], F32)
                                for g in range(WIN_GROUP)], axis=0)
        m = jnp.maximum(jnp.max(s, axis=-1, keepdims=True), sink)
        e = jnp.exp2(s - m)
        den = jnp.sum(e, axis=-1, keepdims=True) + jnp.exp2(sink - m)
        o = jnp.dot(e.astype(BF16), band(v_refs, u, kv), preferred_element_type=F32) / den
        for g in range(WIN_GROUP):
            o_ref[u * BLOCK:(u + 1) * BLOCK, head_cols(kv, g)] = (
                o[g * BLOCK:(g + 1) * BLOCK, :].astype(o_ref.dtype))


def _window_attention(proj, batch, seq, sink, bias_tiles):
    nblk = seq // BLOCK
    steps = nblk // WIN_Q_BLOCKS
    assert nblk % WIN_Q_BLOCKS == 0

    def band_spec(d, blk):
        return pl.BlockSpec(
            (BLOCK, WIN_KV_WIDTH),
            lambda b, i: (b * nblk + jnp.clip(i * WIN_Q_BLOCKS + d, 0, nblk - 1), blk // N_WIN_KV))

    offsets = range(-1, WIN_Q_BLOCKS + 1)
    kernel = functools.partial(_win_attn_kernel, seq=seq)
    q_rows = WIN_Q_BLOCKS * BLOCK
    return pl.pallas_call(
        kernel,
        grid=(batch, steps),
        in_specs=[
            pl.BlockSpec(memory_space=pltpu.SMEM),
            pl.BlockSpec((q_rows, WIN_WIDTH), lambda b, i: (b * steps + i, QW_COL // WIN_WIDTH)),
            *[band_spec(d, KW_BLK) for d in offsets],
            *[band_spec(d, VW_BLK) for d in offsets],
            pl.BlockSpec((N_WIN_KV, WIN_GROUP * BLOCK, 3 * BLOCK), lambda b, i: (0, 0, 0)),
        ],
        out_specs=pl.BlockSpec((q_rows, WIN_WIDTH), lambda b, i: (b * steps + i, 0)),
        out_shape=jax.ShapeDtypeStruct((batch * seq, WIN_WIDTH), BF16),
        compiler_params=_params(("arbitrary", "arbitrary")),
        name="win_attn",
    )(sink.astype(F32) * LOG2E, proj, *([proj] * (2 * len(offsets))), bias_tiles)


def _out_proj_kernel(x_ref, oa_ref, ow_ref, wa_ref, ww_ref, o_ref):
    acc = jnp.dot(oa_ref[...], wa_ref[...], preferred_element_type=F32)
    acc = acc + jnp.dot(ow_ref[...], ww_ref[...], preferred_element_type=F32)
    o_ref[...] = x_ref[...] + acc


def _out_proj(x, oa, ow, wa, ww):
    n_tok, d = x.shape
    tm = min(ROW_TILE_OUT, n_tok)
    return pl.pallas_call(
        _out_proj_kernel,
        grid=(n_tok // tm,),
        in_specs=[
            pl.BlockSpec((tm, d), lambda i: (i, 0)),
            pl.BlockSpec((tm, DIFF_WIDTH), lambda i: (i, 0)),
            pl.BlockSpec((tm, WIN_WIDTH), lambda i: (i, 0)),
            pl.BlockSpec((DIFF_WIDTH, d), lambda i: (0, 0)),
            pl.BlockSpec((WIN_WIDTH, d), lambda i: (0, 0)),
        ],
        out_specs=pl.BlockSpec((tm, d), lambda i: (i, 0)),
        out_shape=jax.ShapeDtypeStruct((n_tok, d), F32),
        compiler_params=_params(("arbitrary",)),
        name="out_proj",
    )(x, oa, ow, wa, ww)


def _ffn_kernel(x_ref, g_ref, w1_ref, w2_ref, fg_ref, o_ref, h_ref, *, final_norm):
    f = pl.program_id(1)

    @pl.when(f == 0)
    def _():
        _norm_into(x_ref, g_ref, h_ref, copy_ref=o_ref)

    rows = h_ref.shape[0] // FF_ROW_GROUPS
    groups = [pl.ds(i * rows, rows) for i in range(FF_ROW_GROUPS)]
    hidden = [jnp.dot(h_ref[grp, :], w1_ref[...], preferred_element_type=F32) for grp in groups]
    for grp, a in zip(groups, hidden):
        a = jnp.square(jnp.maximum(a, 0.0)).astype(BF16)
        o_ref[grp, :] += jnp.dot(a, w2_ref[...], preferred_element_type=F32)

    if final_norm:
        @pl.when(f == pl.num_programs(1) - 1)
        def _():
            _norm_into(o_ref, fg_ref, o_ref)


def _ffn(x, g, w1, w2, final_g, final_norm):
    n_tok, d = x.shape
    d_ff = w1.shape[1]
    tm = min(ROW_TILE_FF, n_tok)
    kernel = functools.partial(_ffn_kernel, final_norm=final_norm)
    return pl.pallas_call(
        kernel,
        grid=(n_tok // tm, d_ff // FF_TILE),
        in_specs=[
            pl.BlockSpec((tm, d), lambda i, f: (i, 0)),
            pl.BlockSpec((1, d), lambda i, f: (0, 0)),
            pl.BlockSpec((d, FF_TILE), lambda i, f: (0, f)),
            pl.BlockSpec((FF_TILE, d), lambda i, f: (f, 0)),
            pl.BlockSpec((1, d), lambda i, f: (0, 0)),
        ],
        out_specs=pl.BlockSpec((tm, d), lambda i, f: (i, 0)),
        out_shape=jax.ShapeDtypeStruct((n_tok, d), F32),
        scratch_shapes=[pltpu.VMEM((tm, d), BF16)],
        compiler_params=_params(("arbitrary", "arbitrary")),
        name="ffn",
    )(x, g.reshape(1, d), w1, w2, final_g.reshape(1, d))


def _bucket_lookup(table, bucket):
    bucket = jnp.asarray(bucket.astype(np.int8))
    cols = table.T.astype(F32).reshape((table.shape[1],) + (1,) * bucket.ndim + (N_BUCKETS,))
    out = jnp.zeros((table.shape[1],) + bucket.shape, F32)
    for b in range(N_BUCKETS):
        out = jnp.where(bucket == b, cols[..., b], out)
    return out


def _diff_bias_tiles(table_a):
    t = ATT_TILE
    k = np.arange(t)[:, None]
    q = np.arange(t)[None, :]
    bucket = np.stack([_t5_bucket_np((o - 1) * t + k - q) for o in range(3)]
                      + [np.full((t, t), N_BUCKETS - 1), np.full((t, t), N_BUCKETS // 2 - 1)])
    assert bucket.shape[0] == N_BIAS_TILES
    return _bucket_lookup(table_a * LOG2E, bucket)


def _win_bias_tiles(table_b):
    rel = np.arange(3 * BLOCK)[None, :] - WINDOW - np.arange(BLOCK)[:, None]
    tiles = _bucket_lookup(table_b * LOG2E, _t5_bucket_np(rel))
    tiles = jnp.where(jnp.asarray(np.abs(rel) <= WINDOW), tiles, NEG)
    return tiles.reshape(N_WIN_KV, WIN_GROUP * BLOCK, 3 * BLOCK)


def _trunk(x3, diff_bias, win_bias, cfar, norm1_g, w_in, lambda_q1, lambda_k1, lambda_q2, lambda_k2,
           diff_subln_g, sink_logit, w_out, norm2_g, w_ff_in, w_ff_out, final_norm_g):
    batch, seq, d = x3.shape
    x = x3.reshape(batch * seq, d)
    for l in range(DEPTH):
        lambda_init = 0.8 - 0.6 * math.exp(-0.3 * l)
        proj = _in_proj(x, norm1_g[l], w_in[l])
        oa = _diff_attention(proj, batch, seq, cfar, diff_bias, lambda_q1[l], lambda_k1[l],
                             lambda_q2[l], lambda_k2[l], diff_subln_g[l], lambda_init)
        ow = _window_attention(proj, batch, seq, sink_logit[l], win_bias)
        x = _out_proj(x, oa, ow, w_out[l, :DIFF_WIDTH], w_out[l, DIFF_WIDTH:])
        x = _ffn(x, norm2_g[l], w_ff_in[l], w_ff_out[l], final_norm_g, final_norm=(l == DEPTH - 1))
    return x.reshape(batch, seq, d)


def kernel(x_prompt, x_sample, rel_bias, norm1_g, w_in, lambda_q1, lambda_k1, lambda_q2, lambda_k2,
           diff_subln_g, sink_logit, w_out, norm2_g, w_ff_in, w_ff_out, final_norm_g):
    table_a = rel_bias[:, :N_DIFF_HEADS]
    table_b = rel_bias[:, N_DIFF_HEADS:]
    diff_bias = _diff_bias_tiles(table_a)
    win_bias = _win_bias_tiles(table_b)
    cfar = jnp.stack([table_a[N_BUCKETS // 2 - 1], table_a[N_BUCKETS - 1]]).astype(F32) * LOG2E
    args = (diff_bias, win_bias, cfar, norm1_g, w_in.astype(BF16), lambda_q1, lambda_k1, lambda_q2,
            lambda_k2, diff_subln_g, sink_logit, w_out.astype(BF16), norm2_g, w_ff_in.astype(BF16),
            w_ff_out.astype(BF16), final_norm_g)
    return (_trunk(x_prompt, *args), _trunk(x_sample, *args))
```

```python
import functools
import math

import numpy as np
import jax
import jax.numpy as jnp
from jax import lax
from jax.experimental import pallas as pl
from jax.experimental.pallas import tpu as pltpu

D_MODEL = 2048
DEPTH = 2
HEAD_DIM = 128
N_DIFF_HEADS = 8
DIFF_QK_DIM = HEAD_DIM // 2
N_WIN_HEADS = 8
N_WIN_KV = 2
WIN_GROUP = N_WIN_HEADS // N_WIN_KV
WINDOW = 128
BLOCK = 128
N_BUCKETS = 32
MAX_DISTANCE = 128
D_FF = 4 * D_MODEL
EPS = 1e-6
NEG = -1e30
DIFF_WIDTH = N_DIFF_HEADS * HEAD_DIM
WIN_WIDTH = N_WIN_HEADS * HEAD_DIM
WIN_KV_WIDTH = N_WIN_KV * HEAD_DIM
IN_WIDTH = 3 * DIFF_WIDTH + WIN_WIDTH + 2 * WIN_KV_WIDTH

QA_BLK = 0
KA_BLK = DIFF_WIDTH // HEAD_DIM
VA_BLK = 2 * DIFF_WIDTH // HEAD_DIM
QW_COL = 3 * DIFF_WIDTH
KW_BLK = (3 * DIFF_WIDTH + WIN_WIDTH) // HEAD_DIM
VW_BLK = KW_BLK + N_WIN_KV

VMEM_LIMIT_BYTES = 56 * 1024 * 1024

LOG2E = math.log2(math.e)

ATT_TILE = 512
N_BIAS_TILES = 5
SUM_ROWS = 16
FAR_UNROLL = 2
DIFF_ATTN_FLAGS = None
ROW_TILE_IN = 1024
COL_TILE_IN = 1536
ROW_TILE_OUT = 512
ROW_TILE_FF = 1024
FF_TILE = 512
FF_ROW_GROUPS = 2
NORM_ROWS = 128
WIN_Q_BLOCKS = 4

BF16 = jnp.bfloat16
F32 = jnp.float32


def _t5_bucket_np(rel):
    nb = N_BUCKETS // 2
    max_exact = nb // 2
    ret = np.where(rel > 0, nb, 0)
    n = np.abs(rel)
    nf = np.maximum(n, 1).astype(np.float32)
    large = max_exact + (np.log(nf / np.float32(max_exact)) / np.float32(math.log(MAX_DISTANCE / max_exact))
                         * np.float32(nb - max_exact)).astype(np.int32)
    large = np.minimum(large, nb - 1)
    return (ret + np.where(n < max_exact, n, large)).astype(np.int32)


_FAR = int(np.min(np.nonzero(_t5_bucket_np(-np.arange(0, 4 * MAX_DISTANCE)) == N_BUCKETS // 2 - 1)[0]))
assert np.all(_t5_bucket_np(-np.arange(_FAR, 1 << 16)) == N_BUCKETS // 2 - 1)
assert np.all(_t5_bucket_np(np.arange(_FAR, 1 << 16)) == N_BUCKETS - 1)
assert ATT_TILE >= _FAR


def _params(semantics, flags=None):
    return pltpu.CompilerParams(dimension_semantics=semantics, vmem_limit_bytes=VMEM_LIMIT_BYTES,
                                flags=flags)


def _rmsnorm_rows(x, g):
    ms = jnp.mean(x * x, axis=-1, keepdims=True)
    return x * lax.rsqrt(ms + EPS) * g


def _norm_into(x_ref, g_ref, h_ref, copy_ref=None):
    rows = x_ref.shape[0]

    def body(r, carry):
        r0 = pl.multiple_of(r * NORM_ROWS, NORM_ROWS)
        x = x_ref[pl.ds(r0, NORM_ROWS), :]
        h_ref[pl.ds(r0, NORM_ROWS), :] = _rmsnorm_rows(x, g_ref[...]).astype(h_ref.dtype)
        if copy_ref is not None:
            copy_ref[pl.ds(r0, NORM_ROWS), :] = x
        return carry

    lax.fori_loop(0, rows // NORM_ROWS, body, 0)


def _in_proj_kernel(x_ref, g_ref, w_ref, cs_ref, o_ref, h_ref):
    @pl.when(pl.program_id(1) == 0)
    def _():
        _norm_into(x_ref, g_ref, h_ref)

    acc = jnp.dot(h_ref[...], w_ref[...], preferred_element_type=F32)
    o_ref[...] = (acc * cs_ref[...]).astype(o_ref.dtype)


def _in_proj_col_scale():
    cs = np.ones((1, IN_WIDTH), np.float32)
    cs[:, :DIFF_WIDTH] = LOG2E * DIFF_QK_DIM ** -0.5
    cs[:, QW_COL:QW_COL + WIN_WIDTH] = LOG2E * HEAD_DIM ** -0.5
    return jnp.asarray(cs)


def _in_proj(x, g, w):
    n_tok, d = x.shape
    n_out = w.shape[1]
    tm = min(ROW_TILE_IN, n_tok)
    return pl.pallas_call(
        _in_proj_kernel,
        grid=(n_tok // tm, n_out // COL_TILE_IN),
        in_specs=[
            pl.BlockSpec((tm, d), lambda i, j: (i, 0)),
            pl.BlockSpec((1, d), lambda i, j: (0, 0)),
            pl.BlockSpec((d, COL_TILE_IN), lambda i, j: (0, j)),
            pl.BlockSpec((1, COL_TILE_IN), lambda i, j: (0, j)),
        ],
        out_specs=pl.BlockSpec((tm, COL_TILE_IN), lambda i, j: (i, j)),
        out_shape=jax.ShapeDtypeStruct((n_tok, n_out), BF16),
        scratch_shapes=[pltpu.VMEM((tm, d), BF16)],
        compiler_params=_params(("arbitrary", "arbitrary")),
        name="in_proj",
    )(x, g.reshape(1, d), w, _in_proj_col_scale())


def _diff_attn_kernel(cfar_ref, zero_ref, q_ref, k_ref, v_ref, bias_ref, lq1_ref, lk1_ref, lq2_ref, lk2_ref,
                      g_ref, o_ref, vt_ref, qt_ref, s00_ref, s01_ref, s10_ref, s11_ref,
                      p00_ref, p01_ref, p10_ref, p11_ref, smax_ref,
                      m_ref, alpha_ref, acc_ref, *, n_chunks, lambda_init):
    t = ATT_TILE
    h = pl.program_id(1)
    qi = pl.program_id(2)
    s_refs = ((s00_ref, s01_ref), (s10_ref, s11_ref))
    p_refs = ((p00_ref, p01_ref), (p10_ref, p11_ref))
    half_cols = (slice(0, t), slice(t, 2 * t))

    @pl.when(qi == 0)
    def _():
        def body(j, carry):
            r0 = pl.multiple_of(j * t, t)
            vt_ref[j, :HEAD_DIM] = v_ref[pl.ds(r0, t), :].astype(F32).T.astype(BF16)
            vt_ref[j, HEAD_DIM:] = jnp.ones((SUM_ROWS, t), BF16)
            return carry

        lax.fori_loop(0, n_chunks, body, 0)

    qt = q_ref[...].astype(F32).T
    row = lax.broadcasted_iota(jnp.int32, qt.shape, 0)
    qt_ref[:, :t] = jnp.where(row < DIFF_QK_DIM, qt, 0.0).astype(BF16)
    qt_ref[:, t:] = jnp.where(row >= DIFF_QK_DIM, qt, 0.0).astype(BF16)
    m_ref[...] = jnp.full(m_ref.shape, NEG, F32)
    acc_ref[...] = jnp.zeros(acc_ref.shape, F32)

    def chunk_of(r):
        return lax.rem(qi + r + (n_chunks - 1), n_chunks)

    def qk(j, half, par, tile):
        cols = half_cols[half]
        r0 = pl.multiple_of(j * t, t)
        s = jnp.dot(k_ref[pl.ds(r0, t), :], qt_ref[:, cols], preferred_element_type=F32)
        if tile is not None:
            s = s + bias_ref[tile]
        s_refs[half][par][...] = s
        smax_ref[:, cols] = jnp.max(s, axis=0, keepdims=True)

    def softmax(half, par, c):
        cols = half_cols[half]
        m_old = m_ref[:, cols]
        m_new = jnp.maximum(m_old, smax_ref[:, cols] + c)
        alpha = jnp.exp2(m_old - m_new)
        s = s_refs[half][par][pl.ds(pl.multiple_of(zero_ref[0], t), t), :]
        p_refs[half][par][...] = jnp.exp2(s - (m_new - c)).astype(BF16)
        alpha_ref[:, cols] = alpha
        m_ref[:, cols] = m_new

    def pv(j, half, par):
        cols = half_cols[half]
        acc_ref[:, cols] = alpha_ref[:, cols] * acc_ref[:, cols] + jnp.dot(
            vt_ref[j], p_refs[half][par][...], preferred_element_type=F32)

    def step(r, par, tile, next_tile, c, first=False, last=False):
        j = chunk_of(r)
        qk(j, 1, par, tile)
        if not first:
            pv(chunk_of(r - 1), 1, 1 - par)
        softmax(0, par, c)
        pv(j, 0, par)
        if not last:
            qk(chunk_of(r + 1), 0, 1 - par, next_tile)
        softmax(1, par, c)

    c_lo = cfar_ref[0, h]
    c_hi = cfar_ref[1, h]

    def far_step(r, par, last=False):
        step(r, par, None, None, jnp.where(chunk_of(r) < qi, c_lo, c_hi), last=last)

    def far_group(i, carry):
        for u in range(FAR_UNROLL):
            far_step(4 + FAR_UNROLL * i + u, u % 2)
        return carry

    n_groups = (n_chunks - 4) // FAR_UNROLL
    near_tiles = (jnp.where(qi >= 1, 0, 3), 1, jnp.where(qi <= n_chunks - 2, 2, 4))
    qk(chunk_of(0), 0, 0, near_tiles[0])
    step(0, 0, near_tiles[0], near_tiles[1], 0.0, first=True)
    step(1, 1, near_tiles[1], near_tiles[2], 0.0)
    step(2, 0, near_tiles[2], None, 0.0)
    far_step(3, 1, last=(n_groups == 0))
    if n_groups > 0:
        lax.fori_loop(0, n_groups - 1, far_group, 0)
        for u in range(FAR_UNROLL):
            far_step(4 + FAR_UNROLL * (n_groups - 1) + u, u % 2, last=(u == FAR_UNROLL - 1))
    pv(chunk_of(n_chunks - 1), 1, (n_chunks - 1) % 2)

    inv_l = 1.0 / acc_ref[HEAD_DIM:HEAD_DIM + 1, :]
    acc = acc_ref[:HEAD_DIM, :]
    o0 = acc[:, :t] * inv_l[:, :t]
    o1 = acc[:, t:] * inv_l[:, t:]
    lam = (jnp.exp(jnp.sum(lq1_ref[...] * lk1_ref[...], axis=1, keepdims=True))
           - jnp.exp(jnp.sum(lq2_ref[...] * lk2_ref[...], axis=1, keepdims=True)) + lambda_init)
    ot = o0 - lam * o1
    ms = jnp.mean(ot * ot, axis=0, keepdims=True)
    y = ot * lax.rsqrt(ms + EPS) * g_ref[...] * (1.0 - lambda_init)
    o_ref[...] = y.T.astype(o_ref.dtype)


def _diff_attention(proj, batch, seq, cfar, bias_tiles, lq1, lk1, lq2, lk2, g, lambda_init):
    t = ATT_TILE
    nq = seq // t
    assert seq % t == 0 and nq >= 4 and (nq - 4) % FAR_UNROLL == 0, "4 leading chunks, then groups"
    vec = lambda a: a.reshape(1, DIFF_QK_DIM).astype(F32)
    vec_spec = pl.BlockSpec((1, DIFF_QK_DIM), lambda b, h, qi: (0, 0))
    kernel = functools.partial(_diff_attn_kernel, n_chunks=nq, lambda_init=lambda_init)
    return pl.pallas_call(
        kernel,
        grid=(batch, N_DIFF_HEADS, nq),
        in_specs=[
            pl.BlockSpec(memory_space=pltpu.SMEM),
            pl.BlockSpec(memory_space=pltpu.SMEM),
            pl.BlockSpec((t, HEAD_DIM), lambda b, h, qi: (b * nq + qi, QA_BLK + h)),
            pl.BlockSpec((seq, HEAD_DIM), lambda b, h, qi: (b, KA_BLK + h)),
            pl.BlockSpec((seq, HEAD_DIM), lambda b, h, qi: (b, VA_BLK + h)),
            pl.BlockSpec((None, N_BIAS_TILES, t, t), lambda b, h, qi: (h, 0, 0, 0)),
            vec_spec, vec_spec, vec_spec, vec_spec,
            pl.BlockSpec((HEAD_DIM, 1), lambda b, h, qi: (0, 0)),
        ],
        out_specs=pl.BlockSpec((t, HEAD_DIM), lambda b, h, qi: (b * nq + qi, h)),
        out_shape=jax.ShapeDtypeStruct((batch * seq, DIFF_WIDTH), BF16),
        scratch_shapes=[
            pltpu.VMEM((nq, HEAD_DIM + SUM_ROWS, t), BF16),
            pltpu.VMEM((HEAD_DIM, 2 * t), BF16),
            pltpu.VMEM((t, t), F32),
            pltpu.VMEM((t, t), F32),
            pltpu.VMEM((t, t), F32),
            pltpu.VMEM((t, t), F32),
            pltpu.VMEM((t, t), BF16),
            pltpu.VMEM((t, t), BF16),
            pltpu.VMEM((t, t), BF16),
            pltpu.VMEM((t, t), BF16),
            pltpu.VMEM((1, 2 * t), F32),
            pltpu.VMEM((1, 2 * t), F32),
            pltpu.VMEM((1, 2 * t), F32),
            pltpu.VMEM((HEAD_DIM + SUM_ROWS, 2 * t), F32),
        ],
        compiler_params=_params(("arbitrary", "arbitrary", "arbitrary"), DIFF_ATTN_FLAGS),
        name="diff_attn",
    )(cfar, jnp.zeros((1,), jnp.int32), proj, proj, proj, bias_tiles, vec(lq1), vec(lk1), vec(lq2), vec(lk2),
      g.reshape(HEAD_DIM, 1).astype(F32))


def _win_attn_kernel(sink_ref, q_ref, *refs, seq):
    n_band = WIN_Q_BLOCKS + 2
    k_refs, v_refs = refs[:n_band], refs[n_band:2 * n_band]
    bias_ref, o_ref = refs[2 * n_band], refs[2 * n_band + 1]
    first = pl.program_id(1) * WIN_Q_BLOCKS

    def band(block_refs, u, kv):
        cols = slice(kv * HEAD_DIM, (kv + 1) * HEAD_DIM)
        return jnp.concatenate([r[:, cols] for r in block_refs[u:u + 3]], axis=0)

    def head_cols(kv, g):
        h = kv * WIN_GROUP + g
        return slice(h * HEAD_DIM, (h + 1) * HEAD_DIM)

    chains = [(u, kv) for u in range(WIN_Q_BLOCKS) for kv in range(N_WIN_KV)]
    scores = []
    for u, kv in chains:
        rows = slice(u * BLOCK, (u + 1) * BLOCK)
        q = jnp.concatenate([q_ref[rows, head_cols(kv, g)] for g in range(WIN_GROUP)], axis=0)
        scores.append(lax.dot_general(q, band(k_refs, u, kv), (((1,), (1,)), ((), ())),
                                      preferred_element_type=F32))
    for (u, kv), s in zip(chains, scores):
        key_pos = (first + u) * BLOCK - WINDOW + lax.broadcasted_iota(jnp.int32, (1, 3 * BLOCK), 1)
        s = jnp.where((key_pos >= 0) & (key_pos < seq), s, NEG) + bias_ref[kv]
        sink = jnp.concatenate([jnp.full((BLOCK, 1), sink_ref[kv * WIN_GROUP + g], F32)
                                for g in range(WIN_GROUP)], axis=0)
        m = jnp.maximum(jnp.max(s, axis=-1, keepdims=True), sink)
        e = jnp.exp2(s - m)
        den = jnp.sum(e, axis=-1, keepdims=True) + jnp.exp2(sink - m)
        o = jnp.dot(e.astype(BF16), band(v_refs, u, kv), preferred_element_type=F32) / den
        for g in range(WIN_GROUP):
            o_ref[u * BLOCK:(u + 1) * BLOCK, head_cols(kv, g)] = (
                o[g * BLOCK:(g + 1) * BLOCK, :].astype(o_ref.dtype))


def _window_attention(proj, batch, seq, sink, bias_tiles):
    nblk = seq // BLOCK
    steps = nblk // WIN_Q_BLOCKS
    assert nblk % WIN_Q_BLOCKS == 0

    def band_spec(d, blk):
        return pl.BlockSpec(
            (BLOCK, WIN_KV_WIDTH),
            lambda b, i: (b * nblk + jnp.clip(i * WIN_Q_BLOCKS + d, 0, nblk - 1), blk // N_WIN_KV))

    offsets = range(-1, WIN_Q_BLOCKS + 1)
    kernel = functools.partial(_win_attn_kernel, seq=seq)
    q_rows = WIN_Q_BLOCKS * BLOCK
    return pl.pallas_call(
        kernel,
        grid=(batch, steps),
        in_specs=[
            pl.BlockSpec(memory_space=pltpu.SMEM),
            pl.BlockSpec((q_rows, WIN_WIDTH), lambda b, i: (b * steps + i, QW_COL // WIN_WIDTH)),
            *[band_spec(d, KW_BLK) for d in offsets],
            *[band_spec(d, VW_BLK) for d in offsets],
            pl.BlockSpec((N_WIN_KV, WIN_GROUP * BLOCK, 3 * BLOCK), lambda b, i: (0, 0, 0)),
        ],
        out_specs=pl.BlockSpec((q_rows, WIN_WIDTH), lambda b, i: (b * steps + i, 0)),
        out_shape=jax.ShapeDtypeStruct((batch * seq, WIN_WIDTH), BF16),
        compiler_params=_params(("arbitrary", "arbitrary")),
        name="win_attn",
    )(sink.astype(F32) * LOG2E, proj, *([proj] * (2 * len(offsets))), bias_tiles)


def _out_proj_kernel(x_ref, oa_ref, ow_ref, wa_ref, ww_ref, o_ref):
    acc = jnp.dot(oa_ref[...], wa_ref[...], preferred_element_type=F32)
    acc = acc + jnp.dot(ow_ref[...], ww_ref[...], preferred_element_type=F32)
    o_ref[...] = x_ref[...] + acc


def _out_proj(x, oa, ow, wa, ww):
    n_tok, d = x.shape
    tm = min(ROW_TILE_OUT, n_tok)
    return pl.pallas_call(
        _out_proj_kernel,
        grid=(n_tok // tm,),
        in_specs=[
            pl.BlockSpec((tm, d), lambda i: (i, 0)),
            pl.BlockSpec((tm, DIFF_WIDTH), lambda i: (i, 0)),
            pl.BlockSpec((tm, WIN_WIDTH), lambda i: (i, 0)),
            pl.BlockSpec((DIFF_WIDTH, d), lambda i: (0, 0)),
            pl.BlockSpec((WIN_WIDTH, d), lambda i: (0, 0)),
        ],
        out_specs=pl.BlockSpec((tm, d), lambda i: (i, 0)),
        out_shape=jax.ShapeDtypeStruct((n_tok, d), F32),
        compiler_params=_params(("arbitrary",)),
        name="out_proj",
    )(x, oa, ow, wa, ww)


def _ffn_kernel(x_ref, g_ref, w1_ref, w2_ref, fg_ref, o_ref, h_ref, *, final_norm):
    f = pl.program_id(1)

    @pl.when(f == 0)
    def _():
        _norm_into(x_ref, g_ref, h_ref, copy_ref=o_ref)

    rows = h_ref.shape[0] // FF_ROW_GROUPS
    groups = [pl.ds(i * rows, rows) for i in range(FF_ROW_GROUPS)]
    hidden = [jnp.dot(h_ref[grp, :], w1_ref[...], preferred_element_type=F32) for grp in groups]
    for grp, a in zip(groups, hidden):
        a = jnp.square(jnp.maximum(a, 0.0)).astype(BF16)
        o_ref[grp, :] += jnp.dot(a, w2_ref[...], preferred_element_type=F32)

    if final_norm:
        @pl.when(f == pl.num_programs(1) - 1)
        def _():
            _norm_into(o_ref, fg_ref, o_ref)


def _ffn(x, g, w1, w2, final_g, final_norm):
    n_tok, d = x.shape
    d_ff = w1.shape[1]
    tm = min(ROW_TILE_FF, n_tok)
    kernel = functools.partial(_ffn_kernel, final_norm=final_norm)
    return pl.pallas_call(
        kernel,
        grid=(n_tok // tm, d_ff // FF_TILE),
        in_specs=[
            pl.BlockSpec((tm, d), lambda i, f: (i, 0)),
            pl.BlockSpec((1, d), lambda i, f: (0, 0)),
            pl.BlockSpec((d, FF_TILE), lambda i, f: (0, f)),
            pl.BlockSpec((FF_TILE, d), lambda i, f: (f, 0)),
            pl.BlockSpec((1, d), lambda i, f: (0, 0)),
        ],
        out_specs=pl.BlockSpec((tm, d), lambda i, f: (i, 0)),
        out_shape=jax.ShapeDtypeStruct((n_tok, d), F32),
        scratch_shapes=[pltpu.VMEM((tm, d), BF16)],
        compiler_params=_params(("arbitrary", "arbitrary")),
        name="ffn",
    )(x, g.reshape(1, d), w1, w2, final_g.reshape(1, d))


def _bucket_lookup(table, bucket):
    bucket = jnp.asarray(bucket.astype(np.int8))
    cols = table.T.astype(F32).reshape((table.shape[1],) + (1,) * bucket.ndim + (N_BUCKETS,))
    out = jnp.zeros((table.shape[1],) + bucket.shape, F32)
    for b in range(N_BUCKETS):
        out = jnp.where(bucket == b, cols[..., b], out)
    return out


def _diff_bias_tiles(table_a):
    t = ATT_TILE
    k = np.arange(t)[:, None]
    q = np.arange(t)[None, :]
    bucket = np.stack([_t5_bucket_np((o - 1) * t + k - q) for o in range(3)]
                      + [np.full((t, t), N_BUCKETS - 1), np.full((t, t), N_BUCKETS // 2 - 1)])
    assert bucket.shape[0] == N_BIAS_TILES
    return _bucket_lookup(table_a * LOG2E, bucket)


def _win_bias_tiles(table_b):
    rel = np.arange(3 * BLOCK)[None, :] - WINDOW - np.arange(BLOCK)[:, None]
    tiles = _bucket_lookup(table_b * LOG2E, _t5_bucket_np(rel))
    tiles = jnp.where(jnp.asarray(np.abs(rel) <= WINDOW), tiles, NEG)
    return tiles.reshape(N_WIN_KV, WIN_GROUP * BLOCK, 3 * BLOCK)


def _trunk(x3, diff_bias, win_bias, cfar, norm1_g, w_in, lambda_q1, lambda_k1, lambda_q2, lambda_k2,
           diff_subln_g, sink_logit, w_out, norm2_g, w_ff_in, w_ff_out, final_norm_g):
    batch, seq, d = x3.shape
    x = x3.reshape(batch * seq, d)
    for l in range(DEPTH):
        lambda_init = 0.8 - 0.6 * math.exp(-0.3 * l)
        proj = _in_proj(x, norm1_g[l], w_in[l])
        oa = _diff_attention(proj, batch, seq, cfar, diff_bias, lambda_q1[l], lambda_k1[l],
                             lambda_q2[l], lambda_k2[l], diff_subln_g[l], lambda_init)
        ow = _window_attention(proj, batch, seq, sink_logit[l], win_bias)
        x = _out_proj(x, oa, ow, w_out[l, :DIFF_WIDTH], w_out[l, DIFF_WIDTH:])
        x = _ffn(x, norm2_g[l], w_ff_in[l], w_ff_out[l], final_norm_g, final_norm=(l == DEPTH - 1))
    return x.reshape(batch, seq, d)


def kernel(x_prompt, x_sample, rel_bias, norm1_g, w_in, lambda_q1, lambda_k1, lambda_q2, lambda_k2,
           diff_subln_g, sink_logit, w_out, norm2_g, w_ff_in, w_ff_out, final_norm_g):
    table_a = rel_bias[:, :N_DIFF_HEADS]
    table_b = rel_bias[:, N_DIFF_HEADS:]
    diff_bias = _diff_bias_tiles(table_a)
    win_bias = _win_bias_tiles(table_b)
    cfar = jnp.stack([table_a[N_BUCKETS // 2 - 1], table_a[N_BUCKETS - 1]]).astype(F32) * LOG2E
    args = (diff_bias, win_bias, cfar, norm1_g, w_in.astype(BF16), lambda_q1, lambda_k1, lambda_q2,
            lambda_k2, diff_subln_g, sink_logit, w_out.astype(BF16), norm2_g, w_ff_in.astype(BF16),
            w_ff_out.astype(BF16), final_norm_g)
    return (_trunk(x_prompt, *args), _trunk(x_sample, *args))
```

```python
import functools
import math

import numpy as np
import jax
import jax.numpy as jnp
from jax import lax
from jax.experimental import pallas as pl
from jax.experimental.pallas import tpu as pltpu

D_MODEL = 2048
DEPTH = 2
HEAD_DIM = 128
N_DIFF_HEADS = 8
DIFF_QK_DIM = HEAD_DIM // 2
N_WIN_HEADS = 8
N_WIN_KV = 2
WIN_GROUP = N_WIN_HEADS // N_WIN_KV
WINDOW = 128
BLOCK = 128
N_BUCKETS = 32
MAX_DISTANCE = 128
D_FF = 4 * D_MODEL
EPS = 1e-6
NEG = -1e30
DIFF_WIDTH = N_DIFF_HEADS * HEAD_DIM
WIN_WIDTH = N_WIN_HEADS * HEAD_DIM
WIN_KV_WIDTH = N_WIN_KV * HEAD_DIM
IN_WIDTH = 3 * DIFF_WIDTH + WIN_WIDTH + 2 * WIN_KV_WIDTH

QA_BLK = 0
KA_BLK = DIFF_WIDTH // HEAD_DIM
VA_BLK = 2 * DIFF_WIDTH // HEAD_DIM
QW_COL = 3 * DIFF_WIDTH
KW_BLK = (3 * DIFF_WIDTH + WIN_WIDTH) // HEAD_DIM
VW_BLK = KW_BLK + N_WIN_KV

VMEM_LIMIT_BYTES = 56 * 1024 * 1024

LOG2E = math.log2(math.e)

ATT_TILE = 512
N_BIAS_TILES = 5
SUM_ROWS = 16
FAR_UNROLL = 2
DIFF_ATTN_FLAGS = None
ROW_TILE_IN = 1024
COL_TILE_IN = 1536
ROW_TILE_OUT = 512
ROW_TILE_FF = 1024
FF_TILE = 512
FF_ROW_GROUPS = 2
NORM_ROWS = 128
WIN_Q_BLOCKS = 4

BF16 = jnp.bfloat16
F32 = jnp.float32


def _t5_bucket_np(rel):
    nb = N_BUCKETS // 2
    max_exact = nb // 2
    ret = np.where(rel > 0, nb, 0)
    n = np.abs(rel)
    nf = np.maximum(n, 1).astype(np.float32)
    large = max_exact + (np.log(nf / np.float32(max_exact)) / np.float32(math.log(MAX_DISTANCE / max_exact))
                         * np.float32(nb - max_exact)).astype(np.int32)
    large = np.minimum(large, nb - 1)
    return (ret + np.where(n < max_exact, n, large)).astype(np.int32)


_FAR = int(np.min(np.nonzero(_t5_bucket_np(-np.arange(0, 4 * MAX_DISTANCE)) == N_BUCKETS // 2 - 1)[0]))
assert np.all(_t5_bucket_np(-np.arange(_FAR, 1 << 16)) == N_BUCKETS // 2 - 1)
assert np.all(_t5_bucket_np(np.arange(_FAR, 1 << 16)) == N_BUCKETS - 1)
assert ATT_TILE >= _FAR


def _params(semantics, flags=None):
    return pltpu.CompilerParams(dimension_semantics=semantics, vmem_limit_bytes=VMEM_LIMIT_BYTES,
                                flags=flags)


def _rmsnorm_rows(x, g):
    ms = jnp.mean(x * x, axis=-1, keepdims=True)
    return x * lax.rsqrt(ms + EPS) * g


def _norm_into(x_ref, g_ref, h_ref, copy_ref=None):
    rows = x_ref.shape[0]

    def body(r, carry):
        r0 = pl.multiple_of(r * NORM_ROWS, NORM_ROWS)
        x = x_ref[pl.ds(r0, NORM_ROWS), :]
        h_ref[pl.ds(r0, NORM_ROWS), :] = _rmsnorm_rows(x, g_ref[...]).astype(h_ref.dtype)
        if copy_ref is not None:
            copy_ref[pl.ds(r0, NORM_ROWS), :] = x
        return carry

    lax.fori_loop(0, rows // NORM_ROWS, body, 0)


def _in_proj_kernel(x_ref, g_ref, w_ref, cs_ref, o_ref, h_ref):
    @pl.when(pl.program_id(1) == 0)
    def _():
        _norm_into(x_ref, g_ref, h_ref)

    acc = jnp.dot(h_ref[...], w_ref[...], preferred_element_type=F32)
    o_ref[...] = (acc * cs_ref[...]).astype(o_ref.dtype)


def _in_proj_col_scale():
    cs = np.ones((1, IN_WIDTH), np.float32)
    cs[:, :DIFF_WIDTH] = LOG2E * DIFF_QK_DIM ** -0.5
    cs[:, QW_COL:QW_COL + WIN_WIDTH] = LOG2E * HEAD_DIM ** -0.5
    return jnp.asarray(cs)


def _in_proj(x, g, w):
    n_tok, d = x.shape
    n_out = w.shape[1]
    tm = min(ROW_TILE_IN, n_tok)
    return pl.pallas_call(
        _in_proj_kernel,
        grid=(n_tok // tm, n_out // COL_TILE_IN),
        in_specs=[
            pl.BlockSpec((tm, d), lambda i, j: (i, 0)),
            pl.BlockSpec((1, d), lambda i, j: (0, 0)),
            pl.BlockSpec((d, COL_TILE_IN), lambda i, j: (0, j)),
            pl.BlockSpec((1, COL_TILE_IN), lambda i, j: (0, j)),
        ],
        out_specs=pl.BlockSpec((tm, COL_TILE_IN), lambda i, j: (i, j)),
        out_shape=jax.ShapeDtypeStruct((n_tok, n_out), BF16),
        scratch_shapes=[pltpu.VMEM((tm, d), BF16)],
        compiler_params=_params(("arbitrary", "arbitrary")),
        name="in_proj",
    )(x, g.reshape(1, d), w, _in_proj_col_scale())


def _diff_attn_kernel(cfar_ref, zero_ref, q_ref, k_ref, v_ref, bias_ref, lq1_ref, lk1_ref, lq2_ref, lk2_ref,
                      g_ref, o_ref, vt_ref, qt_ref, s00_ref, s01_ref, s10_ref, s11_ref,
                      p00_ref, p01_ref, p10_ref, p11_ref, smax_ref,
                      m_ref, alpha_ref, acc_ref, *, n_chunks, lambda_init):
    t = ATT_TILE
    h = pl.program_id(1)
    qi = pl.program_id(2)
    s_refs = ((s00_ref, s01_ref), (s10_ref, s11_ref))
    p_refs = ((p00_ref, p01_ref), (p10_ref, p11_ref))
    half_cols = (slice(0, t), slice(t, 2 * t))

    @pl.when(qi == 0)
    def _():
        def body(j, carry):
            r0 = pl.multiple_of(j * t, t)
            vt_ref[j, :HEAD_DIM] = v_ref[pl.ds(r0, t), :].astype(F32).T.astype(BF16)
            vt_ref[j, HEAD_DIM:] = jnp.ones((SUM_ROWS, t), BF16)
            return carry

        lax.fori_loop(0, n_chunks, body, 0)

    qt = q_ref[...].astype(F32).T
    row = lax.broadcasted_iota(jnp.int32, qt.shape, 0)
    qt_ref[:, :t] = jnp.where(row < DIFF_QK_DIM, qt, 0.0).astype(BF16)
    qt_ref[:, t:] = jnp.where(row >= DIFF_QK_DIM, qt, 0.0).astype(BF16)
    m_ref[...] = jnp.full(m_ref.shape, NEG, F32)
    acc_ref[...] = jnp.zeros(acc_ref.shape, F32)

    def chunk_of(r):
        return lax.rem(qi + r + (n_chunks - 1), n_chunks)

    def qk(j, half, par, tile):
        cols = half_cols[half]
        r0 = pl.multiple_of(j * t, t)
        s = jnp.dot(k_ref[pl.ds(r0, t), :], qt_ref[:, cols], preferred_element_type=F32)
        if tile is not None:
            s = s + bias_ref[tile]
        s_refs[half][par][...] = s
        smax_ref[par, :, cols] = jnp.max(s, axis=0, keepdims=True)

    def softmax(half, par, c):
        cols = half_cols[half]
        m_old = m_ref[:, cols]
        m_new = jnp.maximum(m_old, smax_ref[par, :, cols] + c)
        alpha = jnp.exp2(m_old - m_new)
        s = s_refs[half][par][pl.ds(pl.multiple_of(zero_ref[0], t), t), :]
        p_refs[half][par][...] = jnp.exp2(s - (m_new - c)).astype(BF16)
        alpha_ref[:, cols] = alpha
        m_ref[:, cols] = m_new

    def pv(j, half, par):
        cols = half_cols[half]
        acc_ref[:, cols] = alpha_ref[:, cols] * acc_ref[:, cols] + jnp.dot(
            vt_ref[j], p_refs[half][par][...], preferred_element_type=F32)

    def step(r, par, tile, next_tile, c, prev_c, first=False, last=False):
        j = chunk_of(r)
        qk(j, 1, par, tile)
        if not first:
            softmax(1, 1 - par, prev_c)
            pv(chunk_of(r - 1), 1, 1 - par)
        if not last:
            qk(chunk_of(r + 1), 0, 1 - par, next_tile)
        softmax(0, par, c)
        pv(j, 0, par)

    c_lo = cfar_ref[0, h]
    c_hi = cfar_ref[1, h]

    def far_c(r):
        return jnp.where(chunk_of(r) < qi, c_lo, c_hi)

    def far_step(r, par, prev_c=None, last=False):
        step(r, par, None, None, far_c(r), far_c(r - 1) if prev_c is None else prev_c, last=last)

    def far_group(i, carry):
        for u in range(FAR_UNROLL):
            far_step(4 + FAR_UNROLL * i + u, u % 2)
        return carry

    n_groups = (n_chunks - 4) // FAR_UNROLL
    near_tiles = (jnp.where(qi >= 1, 0, 3), 1, jnp.where(qi <= n_chunks - 2, 2, 4))
    qk(chunk_of(0), 0, 0, near_tiles[0])
    step(0, 0, near_tiles[0], near_tiles[1], 0.0, 0.0, first=True)
    step(1, 1, near_tiles[1], near_tiles[2], 0.0, 0.0)
    step(2, 0, near_tiles[2], None, 0.0, 0.0)
    far_step(3, 1, prev_c=0.0, last=(n_groups == 0))
    if n_groups > 0:
        lax.fori_loop(0, n_groups - 1, far_group, 0)
        for u in range(FAR_UNROLL):
            far_step(4 + FAR_UNROLL * (n_groups - 1) + u, u % 2, last=(u == FAR_UNROLL - 1))
    last_par = (n_chunks - 1) % 2
    softmax(1, last_par, far_c(n_chunks - 1))
    pv(chunk_of(n_chunks - 1), 1, last_par)

    inv_l = 1.0 / acc_ref[HEAD_DIM:HEAD_DIM + 1, :]
    acc = acc_ref[:HEAD_DIM, :]
    o0 = acc[:, :t] * inv_l[:, :t]
    o1 = acc[:, t:] * inv_l[:, t:]
    lam = (jnp.exp(jnp.sum(lq1_ref[...] * lk1_ref[...], axis=1, keepdims=True))
           - jnp.exp(jnp.sum(lq2_ref[...] * lk2_ref[...], axis=1, keepdims=True)) + lambda_init)
    ot = o0 - lam * o1
    ms = jnp.mean(ot * ot, axis=0, keepdims=True)
    y = ot * lax.rsqrt(ms + EPS) * g_ref[...] * (1.0 - lambda_init)
    o_ref[...] = y.T.astype(o_ref.dtype)


def _diff_attention(proj, batch, seq, cfar, bias_tiles, lq1, lk1, lq2, lk2, g, lambda_init):
    t = ATT_TILE
    nq = seq // t
    assert seq % t == 0 and nq >= 4 and (nq - 4) % FAR_UNROLL == 0, "4 leading chunks, then groups"
    vec = lambda a: a.reshape(1, DIFF_QK_DIM).astype(F32)
    vec_spec = pl.BlockSpec((1, DIFF_QK_DIM), lambda b, h, qi: (0, 0))
    kernel = functools.partial(_diff_attn_kernel, n_chunks=nq, lambda_init=lambda_init)
    return pl.pallas_call(
        kernel,
        grid=(batch, N_DIFF_HEADS, nq),
        in_specs=[
            pl.BlockSpec(memory_space=pltpu.SMEM),
            pl.BlockSpec(memory_space=pltpu.SMEM),
            pl.BlockSpec((t, HEAD_DIM), lambda b, h, qi: (b * nq + qi, QA_BLK + h)),
            pl.BlockSpec((seq, HEAD_DIM), lambda b, h, qi: (b, KA_BLK + h)),
            pl.BlockSpec((seq, HEAD_DIM), lambda b, h, qi: (b, VA_BLK + h)),
            pl.BlockSpec((None, N_BIAS_TILES, t, t), lambda b, h, qi: (h, 0, 0, 0)),
            vec_spec, vec_spec, vec_spec, vec_spec,
            pl.BlockSpec((HEAD_DIM, 1), lambda b, h, qi: (0, 0)),
        ],
        out_specs=pl.BlockSpec((t, HEAD_DIM), lambda b, h, qi: (b * nq + qi, h)),
        out_shape=jax.ShapeDtypeStruct((batch * seq, DIFF_WIDTH), BF16),
        scratch_shapes=[
            pltpu.VMEM((nq, HEAD_DIM + SUM_ROWS, t), BF16),
            pltpu.VMEM((HEAD_DIM, 2 * t), BF16),
            pltpu.VMEM((t, t), F32),
            pltpu.VMEM((t, t), F32),
            pltpu.VMEM((t, t), F32),
            pltpu.VMEM((t, t), F32),
            pltpu.VMEM((t, t), BF16),
            pltpu.VMEM((t, t), BF16),
            pltpu.VMEM((t, t), BF16),
            pltpu.VMEM((t, t), BF16),
            pltpu.VMEM((2, 1, 2 * t), F32),
            pltpu.VMEM((1, 2 * t), F32),
            pltpu.VMEM((1, 2 * t), F32),
            pltpu.VMEM((HEAD_DIM + SUM_ROWS, 2 * t), F32),
        ],
        compiler_params=_params(("arbitrary", "arbitrary", "arbitrary"), DIFF_ATTN_FLAGS),
        name="diff_attn",
    )(cfar, jnp.zeros((1,), jnp.int32), proj, proj, proj, bias_tiles, vec(lq1), vec(lk1), vec(lq2), vec(lk2),
      g.reshape(HEAD_DIM, 1).astype(F32))


def _win_attn_kernel(sink_ref, q_ref, *refs, seq):
    n_band = WIN_Q_BLOCKS + 2
    k_refs, v_refs = refs[:n_band], refs[n_band:2 * n_band]
    bias_ref, o_ref = refs[2 * n_band], refs[2 * n_band + 1]
    first = pl.program_id(1) * WIN_Q_BLOCKS

    def band(block_refs, u, kv):
        cols = slice(kv * HEAD_DIM, (kv + 1) * HEAD_DIM)
        return jnp.concatenate([r[:, cols] for r in block_refs[u:u + 3]], axis=0)

    def head_cols(kv, g):
        h = kv * WIN_GROUP + g
        return slice(h * HEAD_DIM, (h + 1) * HEAD_DIM)

    chains = [(u, kv) for u in range(WIN_Q_BLOCKS) for kv in range(N_WIN_KV)]
    scores = []
    for u, kv in chains:
        rows = slice(u * BLOCK, (u + 1) * BLOCK)
        q = jnp.concatenate([q_ref[rows, head_cols(kv, g)] for g in range(WIN_GROUP)], axis=0)
        scores.append(lax.dot_general(q, band(k_refs, u, kv), (((1,), (1,)), ((), ())),
                                      preferred_element_type=F32))
    for (u, kv), s in zip(chains, scores):
        key_pos = (first + u) * BLOCK - WINDOW + lax.broadcasted_iota(jnp.int32, (1, 3 * BLOCK), 1)
        s = jnp.where((key_pos >= 0) & (key_pos < seq), s, NEG) + bias_ref[kv]
        sink = jnp.concatenate([jnp.full((BLOCK, 1), sink_ref[kv * WIN_GROUP + g], F32)
                                for g in range(WIN_GROUP)], axis=0)
        m = jnp.maximum(jnp.max(s, axis=-1, keepdims=True), sink)
        e = jnp.exp2(s - m)
        den = jnp.sum(e, axis=-1, keepdims=True) + jnp.exp2(sink - m)
        o = jnp.dot(e.astype(BF16), band(v_refs, u, kv), preferred_element_type=F32) / den
        for g in range(WIN_GROUP):
            o_ref[u * BLOCK:(u + 1) * BLOCK, head_cols(kv, g)] = (
                o[g * BLOCK:(g + 1) * BLOCK, :].astype(o_ref.dtype))


def _window_attention(proj, batch, seq, sink, bias_tiles):
    nblk = seq // BLOCK
    steps = nblk // WIN_Q_BLOCKS
    assert nblk % WIN_Q_BLOCKS == 0

    def band_spec(d, blk):
        return pl.BlockSpec(
            (BLOCK, WIN_KV_WIDTH),
            lambda b, i: (b * nblk + jnp.clip(i * WIN_Q_BLOCKS + d, 0, nblk - 1), blk // N_WIN_KV))

    offsets = range(-1, WIN_Q_BLOCKS + 1)
    kernel = functools.partial(_win_attn_kernel, seq=seq)
    q_rows = WIN_Q_BLOCKS * BLOCK
    return pl.pallas_call(
        kernel,
        grid=(batch, steps),
        in_specs=[
            pl.BlockSpec(memory_space=pltpu.SMEM),
            pl.BlockSpec((q_rows, WIN_WIDTH), lambda b, i: (b * steps + i, QW_COL // WIN_WIDTH)),
            *[band_spec(d, KW_BLK) for d in offsets],
            *[band_spec(d, VW_BLK) for d in offsets],
            pl.BlockSpec((N_WIN_KV, WIN_GROUP * BLOCK, 3 * BLOCK), lambda b, i: (0, 0, 0)),
        ],
        out_specs=pl.BlockSpec((q_rows, WIN_WIDTH), lambda b, i: (b * steps + i, 0)),
        out_shape=jax.ShapeDtypeStruct((batch * seq, WIN_WIDTH), BF16),
        compiler_params=_params(("arbitrary", "arbitrary")),
        name="win_attn",
    )(sink.astype(F32) * LOG2E, proj, *([proj] * (2 * len(offsets))), bias_tiles)


def _out_proj_kernel(x_ref, oa_ref, ow_ref, wa_ref, ww_ref, o_ref):
    acc = jnp.dot(oa_ref[...], wa_ref[...], preferred_element_type=F32)
    acc = acc + jnp.dot(ow_ref[...], ww_ref[...], preferred_element_type=F32)
    o_ref[...] = x_ref[...] + acc


def _out_proj(x, oa, ow, wa, ww):
    n_tok, d = x.shape
    tm = min(ROW_TILE_OUT, n_tok)
    return pl.pallas_call(
        _out_proj_kernel,
        grid=(n_tok // tm,),
        in_specs=[
            pl.BlockSpec((tm, d), lambda i: (i, 0)),
            pl.BlockSpec((tm, DIFF_WIDTH), lambda i: (i, 0)),
            pl.BlockSpec((tm, WIN_WIDTH), lambda i: (i, 0)),
            pl.BlockSpec((DIFF_WIDTH, d), lambda i: (0, 0)),
            pl.BlockSpec((WIN_WIDTH, d), lambda i: (0, 0)),
        ],
        out_specs=pl.BlockSpec((tm, d), lambda i: (i, 0)),
        out_shape=jax.ShapeDtypeStruct((n_tok, d), F32),
        compiler_params=_params(("arbitrary",)),
        name="out_proj",
    )(x, oa, ow, wa, ww)


def _ffn_kernel(x_ref, g_ref, w1_ref, w2_ref, fg_ref, o_ref, h_ref, *, final_norm):
    f = pl.program_id(1)

    @pl.when(f == 0)
    def _():
        _norm_into(x_ref, g_ref, h_ref, copy_ref=o_ref)

    rows = h_ref.shape[0] // FF_ROW_GROUPS
    groups = [pl.ds(i * rows, rows) for i in range(FF_ROW_GROUPS)]
    hidden = [jnp.dot(h_ref[grp, :], w1_ref[...], preferred_element_type=F32) for grp in groups]
    for grp, a in zip(groups, hidden):
        a = jnp.square(jnp.maximum(a, 0.0)).astype(BF16)
        o_ref[grp, :] += jnp.dot(a, w2_ref[...], preferred_element_type=F32)

    if final_norm:
        @pl.when(f == pl.num_programs(1) - 1)
        def _():
            _norm_into(o_ref, fg_ref, o_ref)


def _ffn(x, g, w1, w2, final_g, final_norm):
    n_tok, d = x.shape
    d_ff = w1.shape[1]
    tm = min(ROW_TILE_FF, n_tok)
    kernel = functools.partial(_ffn_kernel, final_norm=final_norm)
    return pl.pallas_call(
        kernel,
        grid=(n_tok // tm, d_ff // FF_TILE),
        in_specs=[
            pl.BlockSpec((tm, d), lambda i, f: (i, 0)),
            pl.BlockSpec((1, d), lambda i, f: (0, 0)),
            pl.BlockSpec((d, FF_TILE), lambda i, f: (0, f)),
            pl.BlockSpec((FF_TILE, d), lambda i, f: (f, 0)),
            pl.BlockSpec((1, d), lambda i, f: (0, 0)),
        ],
        out_specs=pl.BlockSpec((tm, d), lambda i, f: (i, 0)),
        out_shape=jax.ShapeDtypeStruct((n_tok, d), F32),
        scratch_shapes=[pltpu.VMEM((tm, d), BF16)],
        compiler_params=_params(("arbitrary", "arbitrary")),
        name="ffn",
    )(x, g.reshape(1, d), w1, w2, final_g.reshape(1, d))


def _bucket_lookup(table, bucket):
    bucket = jnp.asarray(bucket.astype(np.int8))
    cols = table.T.astype(F32).reshape((table.shape[1],) + (1,) * bucket.ndim + (N_BUCKETS,))
    out = jnp.zeros((table.shape[1],) + bucket.shape, F32)
    for b in range(N_BUCKETS):
        out = jnp.where(bucket == b, cols[..., b], out)
    return out


def _diff_bias_tiles(table_a):
    t = ATT_TILE
    k = np.arange(t)[:, None]
    q = np.arange(t)[None, :]
    bucket = np.stack([_t5_bucket_np((o - 1) * t + k - q) for o in range(3)]
                      + [np.full((t, t), N_BUCKETS - 1), np.full((t, t), N_BUCKETS // 2 - 1)])
    assert bucket.shape[0] == N_BIAS_TILES
    return _bucket_lookup(table_a * LOG2E, bucket)


def _win_bias_tiles(table_b):
    rel = np.arange(3 * BLOCK)[None, :] - WINDOW - np.arange(BLOCK)[:, None]
    tiles = _bucket_lookup(table_b * LOG2E, _t5_bucket_np(rel))
    tiles = jnp.where(jnp.asarray(np.abs(rel) <= WINDOW), tiles, NEG)
    return tiles.reshape(N_WIN_KV, WIN_GROUP * BLOCK, 3 * BLOCK)


def _trunk(x3, diff_bias, win_bias, cfar, norm1_g, w_in, lambda_q1, lambda_k1, lambda_q2, lambda_k2,
           diff_subln_g, sink_logit, w_out, norm2_g, w_ff_in, w_ff_out, final_norm_g):
    batch, seq, d = x3.shape
    x = x3.reshape(batch * seq, d)
    for l in range(DEPTH):
        lambda_init = 0.8 - 0.6 * math.exp(-0.3 * l)
        proj = _in_proj(x, norm1_g[l], w_in[l])
        oa = _diff_attention(proj, batch, seq, cfar, diff_bias, lambda_q1[l], lambda_k1[l],
                             lambda_q2[l], lambda_k2[l], diff_subln_g[l], lambda_init)
        ow = _window_attention(proj, batch, seq, sink_logit[l], win_bias)
        x = _out_proj(x, oa, ow, w_out[l, :DIFF_WIDTH], w_out[l, DIFF_WIDTH:])
        x = _ffn(x, norm2_g[l], w_ff_in[l], w_ff_out[l], final_norm_g, final_norm=(l == DEPTH - 1))
    return x.reshape(batch, seq, d)


def kernel(x_prompt, x_sample, rel_bias, norm1_g, w_in, lambda_q1, lambda_k1, lambda_q2, lambda_k2,
           diff_subln_g, sink_logit, w_out, norm2_g, w_ff_in, w_ff_out, final_norm_g):
    table_a = rel_bias[:, :N_DIFF_HEADS]
    table_b = rel_bias[:, N_DIFF_HEADS:]
    diff_bias = _diff_bias_tiles(table_a)
    win_bias = _win_bias_tiles(table_b)
    cfar = jnp.stack([table_a[N_BUCKETS // 2 - 1], table_a[N_BUCKETS - 1]]).astype(F32) * LOG2E
    args = (diff_bias, win_bias, cfar, norm1_g, w_in.astype(BF16), lambda_q1, lambda_k1, lambda_q2,
            lambda_k2, diff_subln_g, sink_logit, w_out.astype(BF16), norm2_g, w_ff_in.astype(BF16),
            w_ff_out.astype(BF16), final_norm_g)
    return (_trunk(x_prompt, *args), _trunk(x_sample, *args))
```

```python
import functools
import math

import numpy as np
import jax
import jax.numpy as jnp
from jax import lax
from jax.experimental import pallas as pl
from jax.experimental.pallas import tpu as pltpu

D_MODEL = 2048
DEPTH = 2
HEAD_DIM = 128
N_DIFF_HEADS = 8
DIFF_QK_DIM = HEAD_DIM // 2
N_WIN_HEADS = 8
N_WIN_KV = 2
WIN_GROUP = N_WIN_HEADS // N_WIN_KV
WINDOW = 128
BLOCK = 128
N_BUCKETS = 32
MAX_DISTANCE = 128
D_FF = 4 * D_MODEL
EPS = 1e-6
NEG = -1e30
DIFF_WIDTH = N_DIFF_HEADS * HEAD_DIM
WIN_WIDTH = N_WIN_HEADS * HEAD_DIM
WIN_KV_WIDTH = N_WIN_KV * HEAD_DIM
IN_WIDTH = 3 * DIFF_WIDTH + WIN_WIDTH + 2 * WIN_KV_WIDTH

QA_BLK = 0
KA_BLK = DIFF_WIDTH // HEAD_DIM
VA_BLK = 2 * DIFF_WIDTH // HEAD_DIM
QW_COL = 3 * DIFF_WIDTH
KW_BLK = (3 * DIFF_WIDTH + WIN_WIDTH) // HEAD_DIM
VW_BLK = KW_BLK + N_WIN_KV

VMEM_LIMIT_BYTES = 56 * 1024 * 1024

LOG2E = math.log2(math.e)

ATT_TILE = 512
N_BIAS_TILES = 5
SUM_ROWS = 16
FAR_UNROLL = 2
DIFF_ATTN_FLAGS = None
ROW_TILE_IN = 1024
COL_TILE_IN = 1536
ROW_TILE_OUT = 512
ROW_TILE_FF = 1024
FF_TILE = 512
FF_ROW_GROUPS = 2
NORM_ROWS = 128
WIN_Q_BLOCKS = 4

BF16 = jnp.bfloat16
F32 = jnp.float32


def _t5_bucket_np(rel):
    nb = N_BUCKETS // 2
    max_exact = nb // 2
    ret = np.where(rel > 0, nb, 0)
    n = np.abs(rel)
    nf = np.maximum(n, 1).astype(np.float32)
    large = max_exact + (np.log(nf / np.float32(max_exact)) / np.float32(math.log(MAX_DISTANCE / max_exact))
                         * np.float32(nb - max_exact)).astype(np.int32)
    large = np.minimum(large, nb - 1)
    return (ret + np.where(n < max_exact, n, large)).astype(np.int32)


_FAR = int(np.min(np.nonzero(_t5_bucket_np(-np.arange(0, 4 * MAX_DISTANCE)) == N_BUCKETS // 2 - 1)[0]))
assert np.all(_t5_bucket_np(-np.arange(_FAR, 1 << 16)) == N_BUCKETS // 2 - 1)
assert np.all(_t5_bucket_np(np.arange(_FAR, 1 << 16)) == N_BUCKETS - 1)
assert ATT_TILE >= _FAR


def _params(semantics, flags=None):
    return pltpu.CompilerParams(dimension_semantics=semantics, vmem_limit_bytes=VMEM_LIMIT_BYTES,
                                flags=flags)


def _rmsnorm_rows(x, g):
    ms = jnp.mean(x * x, axis=-1, keepdims=True)
    return x * lax.rsqrt(ms + EPS) * g


def _norm_into(x_ref, g_ref, h_ref, copy_ref=None):
    rows = x_ref.shape[0]

    def body(r, carry):
        r0 = pl.multiple_of(r * NORM_ROWS, NORM_ROWS)
        x = x_ref[pl.ds(r0, NORM_ROWS), :]
        h_ref[pl.ds(r0, NORM_ROWS), :] = _rmsnorm_rows(x, g_ref[...]).astype(h_ref.dtype)
        if copy_ref is not None:
            copy_ref[pl.ds(r0, NORM_ROWS), :] = x
        return carry

    lax.fori_loop(0, rows // NORM_ROWS, body, 0)


def _in_proj_kernel(x_ref, g_ref, w_ref, cs_ref, o_ref, h_ref):
    @pl.when(pl.program_id(1) == 0)
    def _():
        _norm_into(x_ref, g_ref, h_ref)

    acc = jnp.dot(h_ref[...], w_ref[...], preferred_element_type=F32)
    o_ref[...] = (acc * cs_ref[...]).astype(o_ref.dtype)


def _in_proj_col_scale():
    cs = np.ones((1, IN_WIDTH), np.float32)
    cs[:, :DIFF_WIDTH] = LOG2E * DIFF_QK_DIM ** -0.5
    cs[:, QW_COL:QW_COL + WIN_WIDTH] = LOG2E * HEAD_DIM ** -0.5
    return jnp.asarray(cs)


def _in_proj(x, g, w):
    n_tok, d = x.shape
    n_out = w.shape[1]
    tm = min(ROW_TILE_IN, n_tok)
    return pl.pallas_call(
        _in_proj_kernel,
        grid=(n_tok // tm, n_out // COL_TILE_IN),
        in_specs=[
            pl.BlockSpec((tm, d), lambda i, j: (i, 0)),
            pl.BlockSpec((1, d), lambda i, j: (0, 0)),
            pl.BlockSpec((d, COL_TILE_IN), lambda i, j: (0, j)),
            pl.BlockSpec((1, COL_TILE_IN), lambda i, j: (0, j)),
        ],
        out_specs=pl.BlockSpec((tm, COL_TILE_IN), lambda i, j: (i, j)),
        out_shape=jax.ShapeDtypeStruct((n_tok, n_out), BF16),
        scratch_shapes=[pltpu.VMEM((tm, d), BF16)],
        compiler_params=_params(("arbitrary", "arbitrary")),
        name="in_proj",
    )(x, g.reshape(1, d), w, _in_proj_col_scale())


def _diff_attn_kernel(cfar_ref, zero_ref, q_ref, k_ref, v_ref, bias_ref, lq1_ref, lk1_ref, lq2_ref, lk2_ref,
                      g_ref, o_ref, vt_ref, qt_ref, s00_ref, s01_ref, s10_ref, s11_ref,
                      p00_ref, p01_ref, p10_ref, p11_ref, smax_ref,
                      m_ref, alpha_ref, acc_ref, *, n_chunks, lambda_init):
    t = ATT_TILE
    h = pl.program_id(1)
    qi = pl.program_id(2)
    s_refs = ((s00_ref, s01_ref), (s10_ref, s11_ref))
    p_refs = ((p00_ref, p01_ref), (p10_ref, p11_ref))
    half_cols = (slice(0, t), slice(t, 2 * t))

    @pl.when(qi == 0)
    def _():
        def body(j, carry):
            r0 = pl.multiple_of(j * t, t)
            vt_ref[j, :HEAD_DIM] = v_ref[pl.ds(r0, t), :].astype(F32).T.astype(BF16)
            vt_ref[j, HEAD_DIM:] = jnp.ones((SUM_ROWS, t), BF16)
            return carry

        lax.fori_loop(0, n_chunks, body, 0)

    qt = q_ref[...].astype(F32).T
    row = lax.broadcasted_iota(jnp.int32, qt.shape, 0)
    qt_ref[:, :t] = jnp.where(row < DIFF_QK_DIM, qt, 0.0).astype(BF16)
    qt_ref[:, t:] = jnp.where(row >= DIFF_QK_DIM, qt, 0.0).astype(BF16)
    m_ref[...] = jnp.full(m_ref.shape, NEG, F32)
    acc_ref[...] = jnp.zeros(acc_ref.shape, F32)

    def chunk_of(r):
        return lax.rem(qi + r + (n_chunks - 1), n_chunks)

    def qk(j, half, par, tile):
        cols = half_cols[half]
        r0 = pl.multiple_of(j * t, t)
        s = jnp.dot(k_ref[pl.ds(r0, t), :], qt_ref[:, cols], preferred_element_type=F32)
        if tile is not None:
            s = s + bias_ref[tile]
        s_refs[half][par][...] = s
        smax_ref[par, :, cols] = jnp.max(s, axis=0, keepdims=True)

    def softmax(half, par, c):
        cols = half_cols[half]
        m_old = m_ref[:, cols]
        m_new = jnp.maximum(m_old, smax_ref[par, :, cols] + c)
        alpha = jnp.exp2(m_old - m_new)
        s = s_refs[half][par][pl.ds(pl.multiple_of(zero_ref[0], t), t), :]
        p_refs[half][par][...] = jnp.exp2(s - (m_new - c)).astype(BF16)
        alpha_ref[:, cols] = alpha
        m_ref[:, cols] = m_new

    def pv(j, half, par):
        cols = half_cols[half]
        acc_ref[:, cols] = alpha_ref[:, cols] * acc_ref[:, cols] + jnp.dot(
            vt_ref[j], p_refs[half][par][...], preferred_element_type=F32)

    def step(r, par, tile, next_tile, c, prev_c, first=False, last=False):
        j = chunk_of(r)
        qk(j, 1, par, tile)
        if not first:
            softmax(1, 1 - par, prev_c)
            pv(chunk_of(r - 1), 1, 1 - par)
        if not last:
            qk(chunk_of(r + 1), 0, 1 - par, next_tile)
        softmax(0, par, c)
        pv(j, 0, par)

    c_lo = cfar_ref[0, h]
    c_hi = cfar_ref[1, h]

    def far_c(r):
        return jnp.where(chunk_of(r) < qi, c_lo, c_hi)

    def far_step(r, par, prev_c=None, last=False):
        step(r, par, None, None, far_c(r), far_c(r - 1) if prev_c is None else prev_c, last=last)

    def far_group(i, carry):
        for u in range(FAR_UNROLL):
            far_step(4 + FAR_UNROLL * i + u, u % 2)
        return carry

    n_groups = (n_chunks - 4) // FAR_UNROLL
    near_tiles = (jnp.where(qi >= 1, 0, 3), 1, jnp.where(qi <= n_chunks - 2, 2, 4))
    qk(chunk_of(0), 0, 0, near_tiles[0])
    step(0, 0, near_tiles[0], near_tiles[1], 0.0, 0.0, first=True)
    step(1, 1, near_tiles[1], near_tiles[2], 0.0, 0.0)
    @pl.when(qi >= 0)
    def _():
        step(2, 0, near_tiles[2], None, 0.0, 0.0)
        far_step(3, 1, prev_c=0.0, last=(n_groups == 0))

    if n_groups > 0:
        lax.fori_loop(0, n_groups - 1, far_group, 0)
        for u in range(FAR_UNROLL):
            far_step(4 + FAR_UNROLL * (n_groups - 1) + u, u % 2, last=(u == FAR_UNROLL - 1))
    last_par = (n_chunks - 1) % 2
    softmax(1, last_par, far_c(n_chunks - 1))
    pv(chunk_of(n_chunks - 1), 1, last_par)

    inv_l = 1.0 / acc_ref[HEAD_DIM:HEAD_DIM + 1, :]
    acc = acc_ref[:HEAD_DIM, :]
    o0 = acc[:, :t] * inv_l[:, :t]
    o1 = acc[:, t:] * inv_l[:, t:]
    lam = (jnp.exp(jnp.sum(lq1_ref[...] * lk1_ref[...], axis=1, keepdims=True))
           - jnp.exp(jnp.sum(lq2_ref[...] * lk2_ref[...], axis=1, keepdims=True)) + lambda_init)
    ot = o0 - lam * o1
    ms = jnp.mean(ot * ot, axis=0, keepdims=True)
    y = ot * lax.rsqrt(ms + EPS) * g_ref[...] * (1.0 - lambda_init)
    o_ref[...] = y.T.astype(o_ref.dtype)


def _diff_attention(proj, batch, seq, cfar, bias_tiles, lq1, lk1, lq2, lk2, g, lambda_init):
    t = ATT_TILE
    nq = seq // t
    assert seq % t == 0 and nq >= 4 and (nq - 4) % FAR_UNROLL == 0, "4 leading chunks, then groups"
    vec = lambda a: a.reshape(1, DIFF_QK_DIM).astype(F32)
    vec_spec = pl.BlockSpec((1, DIFF_QK_DIM), lambda b, h, qi: (0, 0))
    kernel = functools.partial(_diff_attn_kernel, n_chunks=nq, lambda_init=lambda_init)
    return pl.pallas_call(
        kernel,
        grid=(batch, N_DIFF_HEADS, nq),
        in_specs=[
            pl.BlockSpec(memory_space=pltpu.SMEM),
            pl.BlockSpec(memory_space=pltpu.SMEM),
            pl.BlockSpec((t, HEAD_DIM), lambda b, h, qi: (b * nq + qi, QA_BLK + h)),
            pl.BlockSpec((seq, HEAD_DIM), lambda b, h, qi: (b, KA_BLK + h)),
            pl.BlockSpec((seq, HEAD_DIM), lambda b, h, qi: (b, VA_BLK + h)),
            pl.BlockSpec((None, N_BIAS_TILES, t, t), lambda b, h, qi: (h, 0, 0, 0)),
            vec_spec, vec_spec, vec_spec, vec_spec,
            pl.BlockSpec((HEAD_DIM, 1), lambda b, h, qi: (0, 0)),
        ],
        out_specs=pl.BlockSpec((t, HEAD_DIM), lambda b, h, qi: (b * nq + qi, h)),
        out_shape=jax.ShapeDtypeStruct((batch * seq, DIFF_WIDTH), BF16),
        scratch_shapes=[
            pltpu.VMEM((nq, HEAD_DIM + SUM_ROWS, t), BF16),
            pltpu.VMEM((HEAD_DIM, 2 * t), BF16),
            pltpu.VMEM((t, t), F32),
            pltpu.VMEM((t, t), F32),
            pltpu.VMEM((t, t), F32),
            pltpu.VMEM((t, t), F32),
            pltpu.VMEM((t, t), BF16),
            pltpu.VMEM((t, t), BF16),
            pltpu.VMEM((t, t), BF16),
            pltpu.VMEM((t, t), BF16),
            pltpu.VMEM((2, 1, 2 * t), F32),
            pltpu.VMEM((1, 2 * t), F32),
            pltpu.VMEM((1, 2 * t), F32),
            pltpu.VMEM((HEAD_DIM + SUM_ROWS, 2 * t), F32),
        ],
        compiler_params=_params(("arbitrary", "arbitrary", "arbitrary"), DIFF_ATTN_FLAGS),
        name="diff_attn",
    )(cfar, jnp.zeros((1,), jnp.int32), proj, proj, proj, bias_tiles, vec(lq1), vec(lk1), vec(lq2), vec(lk2),
      g.reshape(HEAD_DIM, 1).astype(F32))


def _win_attn_kernel(sink_ref, q_ref, *refs, seq):
    n_band = WIN_Q_BLOCKS + 2
    k_refs, v_refs = refs[:n_band], refs[n_band:2 * n_band]
    bias_ref, o_ref = refs[2 * n_band], refs[2 * n_band + 1]
    first = pl.program_id(1) * WIN_Q_BLOCKS

    def band(block_refs, u, kv):
        cols = slice(kv * HEAD_DIM, (kv + 1) * HEAD_DIM)
        return jnp.concatenate([r[:, cols] for r in block_refs[u:u + 3]], axis=0)

    def head_cols(kv, g):
        h = kv * WIN_GROUP + g
        return slice(h * HEAD_DIM, (h + 1) * HEAD_DIM)

    chains = [(u, kv) for u in range(WIN_Q_BLOCKS) for kv in range(N_WIN_KV)]
    scores = []
    for u, kv in chains:
        rows = slice(u * BLOCK, (u + 1) * BLOCK)
        q = jnp.concatenate([q_ref[rows, head_cols(kv, g)] for g in range(WIN_GROUP)], axis=0)
        scores.append(lax.dot_general(q, band(k_refs, u, kv), (((1,), (1,)), ((), ())),
                                      preferred_element_type=F32))
    for (u, kv), s in zip(chains, scores):
        key_pos = (first + u) * BLOCK - WINDOW + lax.broadcasted_iota(jnp.int32, (1, 3 * BLOCK), 1)
        s = jnp.where((key_pos >= 0) & (key_pos < seq), s, NEG) + bias_ref[kv]
        sink = jnp.concatenate([jnp.full((BLOCK, 1), sink_ref[kv * WIN_GROUP + g], F32)
                                for g in range(WIN_GROUP)], axis=0)
        m = jnp.maximum(jnp.max(s, axis=-1, keepdims=True), sink)
        e = jnp.exp2(s - m)
        den = jnp.sum(e, axis=-1, keepdims=True) + jnp.exp2(sink - m)
        o = jnp.dot(e.astype(BF16), band(v_refs, u, kv), preferred_element_type=F32) / den
        for g in range(WIN_GROUP):
            o_ref[u * BLOCK:(u + 1) * BLOCK, head_cols(kv, g)] = (
                o[g * BLOCK:(g + 1) * BLOCK, :].astype(o_ref.dtype))


def _window_attention(proj, batch, seq, sink, bias_tiles):
    nblk = seq // BLOCK
    steps = nblk // WIN_Q_BLOCKS
    assert nblk % WIN_Q_BLOCKS == 0

    def band_spec(d, blk):
        return pl.BlockSpec(
            (BLOCK, WIN_KV_WIDTH),
            lambda b, i: (b * nblk + jnp.clip(i * WIN_Q_BLOCKS + d, 0, nblk - 1), blk // N_WIN_KV))

    offsets = range(-1, WIN_Q_BLOCKS + 1)
    kernel = functools.partial(_win_attn_kernel, seq=seq)
    q_rows = WIN_Q_BLOCKS * BLOCK
    return pl.pallas_call(
        kernel,
        grid=(batch, steps),
        in_specs=[
            pl.BlockSpec(memory_space=pltpu.SMEM),
            pl.BlockSpec((q_rows, WIN_WIDTH), lambda b, i: (b * steps + i, QW_COL // WIN_WIDTH)),
            *[band_spec(d, KW_BLK) for d in offsets],
            *[band_spec(d, VW_BLK) for d in offsets],
            pl.BlockSpec((N_WIN_KV, WIN_GROUP * BLOCK, 3 * BLOCK), lambda b, i: (0, 0, 0)),
        ],
        out_specs=pl.BlockSpec((q_rows, WIN_WIDTH), lambda b, i: (b * steps + i, 0)),
        out_shape=jax.ShapeDtypeStruct((batch * seq, WIN_WIDTH), BF16),
        compiler_params=_params(("arbitrary", "arbitrary")),
        name="win_attn",
    )(sink.astype(F32) * LOG2E, proj, *([proj] * (2 * len(offsets))), bias_tiles)


def _out_proj_kernel(x_ref, oa_ref, ow_ref, wa_ref, ww_ref, o_ref):
    acc = jnp.dot(oa_ref[...], wa_ref[...], preferred_element_type=F32)
    acc = acc + jnp.dot(ow_ref[...], ww_ref[...], preferred_element_type=F32)
    o_ref[...] = x_ref[...] + acc


def _out_proj(x, oa, ow, wa, ww):
    n_tok, d = x.shape
    tm = min(ROW_TILE_OUT, n_tok)
    return pl.pallas_call(
        _out_proj_kernel,
        grid=(n_tok // tm,),
        in_specs=[
            pl.BlockSpec((tm, d), lambda i: (i, 0)),
            pl.BlockSpec((tm, DIFF_WIDTH), lambda i: (i, 0)),
            pl.BlockSpec((tm, WIN_WIDTH), lambda i: (i, 0)),
            pl.BlockSpec((DIFF_WIDTH, d), lambda i: (0, 0)),
            pl.BlockSpec((WIN_WIDTH, d), lambda i: (0, 0)),
        ],
        out_specs=pl.BlockSpec((tm, d), lambda i: (i, 0)),
        out_shape=jax.ShapeDtypeStruct((n_tok, d), F32),
        compiler_params=_params(("arbitrary",)),
        name="out_proj",
    )(x, oa, ow, wa, ww)


def _ffn_kernel(x_ref, g_ref, w1_ref, w2_ref, fg_ref, o_ref, h_ref, *, final_norm):
    f = pl.program_id(1)

    @pl.when(f == 0)
    def _():
        _norm_into(x_ref, g_ref, h_ref, copy_ref=o_ref)

    rows = h_ref.shape[0] // FF_ROW_GROUPS
    groups = [pl.ds(i * rows, rows) for i in range(FF_ROW_GROUPS)]
    hidden = [jnp.dot(h_ref[grp, :], w1_ref[...], preferred_element_type=F32) for grp in groups]
    for grp, a in zip(groups, hidden):
        a = jnp.square(jnp.maximum(a, 0.0)).astype(BF16)
        o_ref[grp, :] += jnp.dot(a, w2_ref[...], preferred_element_type=F32)

    if final_norm:
        @pl.when(f == pl.num_programs(1) - 1)
        def _():
            _norm_into(o_ref, fg_ref, o_ref)


def _ffn(x, g, w1, w2, final_g, final_norm):
    n_tok, d = x.shape
    d_ff = w1.shape[1]
    tm = min(ROW_TILE_FF, n_tok)
    kernel = functools.partial(_ffn_kernel, final_norm=final_norm)
    return pl.pallas_call(
        kernel,
        grid=(n_tok // tm, d_ff // FF_TILE),
        in_specs=[
            pl.BlockSpec((tm, d), lambda i, f: (i, 0)),
            pl.BlockSpec((1, d), lambda i, f: (0, 0)),
            pl.BlockSpec((d, FF_TILE), lambda i, f: (0, f)),
            pl.BlockSpec((FF_TILE, d), lambda i, f: (f, 0)),
            pl.BlockSpec((1, d), lambda i, f: (0, 0)),
        ],
        out_specs=pl.BlockSpec((tm, d), lambda i, f: (i, 0)),
        out_shape=jax.ShapeDtypeStruct((n_tok, d), F32),
        scratch_shapes=[pltpu.VMEM((tm, d), BF16)],
        compiler_params=_params(("arbitrary", "arbitrary")),
        name="ffn",
    )(x, g.reshape(1, d), w1, w2, final_g.reshape(1, d))


def _bucket_lookup(table, bucket):
    bucket = jnp.asarray(bucket.astype(np.int8))
    cols = table.T.astype(F32).reshape((table.shape[1],) + (1,) * bucket.ndim + (N_BUCKETS,))
    out = jnp.zeros((table.shape[1],) + bucket.shape, F32)
    for b in range(N_BUCKETS):
        out = jnp.where(bucket == b, cols[..., b], out)
    return out


def _diff_bias_tiles(table_a):
    t = ATT_TILE
    k = np.arange(t)[:, None]
    q = np.arange(t)[None, :]
    bucket = np.stack([_t5_bucket_np((o - 1) * t + k - q) for o in range(3)]
                      + [np.full((t, t), N_BUCKETS - 1), np.full((t, t), N_BUCKETS // 2 - 1)])
    assert bucket.shape[0] == N_BIAS_TILES
    return _bucket_lookup(table_a * LOG2E, bucket)


def _win_bias_tiles(table_b):
    rel = np.arange(3 * BLOCK)[None, :] - WINDOW - np.arange(BLOCK)[:, None]
    tiles = _bucket_lookup(table_b * LOG2E, _t5_bucket_np(rel))
    tiles = jnp.where(jnp.asarray(np.abs(rel) <= WINDOW), tiles, NEG)
    return tiles.reshape(N_WIN_KV, WIN_GROUP * BLOCK, 3 * BLOCK)


def _trunk(x3, diff_bias, win_bias, cfar, norm1_g, w_in, lambda_q1, lambda_k1, lambda_q2, lambda_k2,
           diff_subln_g, sink_logit, w_out, norm2_g, w_ff_in, w_ff_out, final_norm_g):
    batch, seq, d = x3.shape
    x = x3.reshape(batch * seq, d)
    for l in range(DEPTH):
        lambda_init = 0.8 - 0.6 * math.exp(-0.3 * l)
        proj = _in_proj(x, norm1_g[l], w_in[l])
        oa = _diff_attention(proj, batch, seq, cfar, diff_bias, lambda_q1[l], lambda_k1[l],
                             lambda_q2[l], lambda_k2[l], diff_subln_g[l], lambda_init)
        ow = _window_attention(proj, batch, seq, sink_logit[l], win_bias)
        x = _out_proj(x, oa, ow, w_out[l, :DIFF_WIDTH], w_out[l, DIFF_WIDTH:])
        x = _ffn(x, norm2_g[l], w_ff_in[l], w_ff_out[l], final_norm_g, final_norm=(l == DEPTH - 1))
    return x.reshape(batch, seq, d)


def kernel(x_prompt, x_sample, rel_bias, norm1_g, w_in, lambda_q1, lambda_k1, lambda_q2, lambda_k2,
           diff_subln_g, sink_logit, w_out, norm2_g, w_ff_in, w_ff_out, final_norm_g):
    table_a = rel_bias[:, :N_DIFF_HEADS]
    table_b = rel_bias[:, N_DIFF_HEADS:]
    diff_bias = _diff_bias_tiles(table_a)
    win_bias = _win_bias_tiles(table_b)
    cfar = jnp.stack([table_a[N_BUCKETS // 2 - 1], table_a[N_BUCKETS - 1]]).astype(F32) * LOG2E
    args = (diff_bias, win_bias, cfar, norm1_g, w_in.astype(BF16), lambda_q1, lambda_k1, lambda_q2,
            lambda_k2, diff_subln_g, sink_logit, w_out.astype(BF16), norm2_g, w_ff_in.astype(BF16),
            w_ff_out.astype(BF16), final_norm_g)
    return (_trunk(x_prompt, *args), _trunk(x_sample, *args))
```

```python
import functools
import math

import numpy as np
import jax
import jax.numpy as jnp
from jax import lax
from jax.experimental import pallas as pl
from jax.experimental.pallas import tpu as pltpu

D_MODEL = 2048
DEPTH = 2
HEAD_DIM = 128
N_DIFF_HEADS = 8
DIFF_QK_DIM = HEAD_DIM // 2
N_WIN_HEADS = 8
N_WIN_KV = 2
WIN_GROUP = N_WIN_HEADS // N_WIN_KV
WINDOW = 128
BLOCK = 128
N_BUCKETS = 32
MAX_DISTANCE = 128
D_FF = 4 * D_MODEL
EPS = 1e-6
NEG = -1e30
DIFF_WIDTH = N_DIFF_HEADS * HEAD_DIM
WIN_WIDTH = N_WIN_HEADS * HEAD_DIM
WIN_KV_WIDTH = N_WIN_KV * HEAD_DIM
IN_WIDTH = 3 * DIFF_WIDTH + WIN_WIDTH + 2 * WIN_KV_WIDTH

QA_BLK = 0
KA_BLK = DIFF_WIDTH // HEAD_DIM
VA_BLK = 2 * DIFF_WIDTH // HEAD_DIM
QW_COL = 3 * DIFF_WIDTH
KW_BLK = (3 * DIFF_WIDTH + WIN_WIDTH) // HEAD_DIM
VW_BLK = KW_BLK + N_WIN_KV

VMEM_LIMIT_BYTES = 56 * 1024 * 1024

LOG2E = math.log2(math.e)

ATT_TILE = 512
N_BIAS_TILES = 5
SUM_ROWS = 16
FAR_UNROLL = 2
DIFF_ATTN_FLAGS = None
ROW_TILE_IN = 1024
COL_TILE_IN = 1536
ROW_TILE_OUT = 512
ROW_TILE_FF = 1024
FF_TILE = 512
FF_ROW_GROUPS = 2
NORM_ROWS = 128
WIN_Q_BLOCKS = 4

BF16 = jnp.bfloat16
F32 = jnp.float32


def _t5_bucket_np(rel):
    nb = N_BUCKETS // 2
    max_exact = nb // 2
    ret = np.where(rel > 0, nb, 0)
    n = np.abs(rel)
    nf = np.maximum(n, 1).astype(np.float32)
    large = max_exact + (np.log(nf / np.float32(max_exact)) / np.float32(math.log(MAX_DISTANCE / max_exact))
                         * np.float32(nb - max_exact)).astype(np.int32)
    large = np.minimum(large, nb - 1)
    return (ret + np.where(n < max_exact, n, large)).astype(np.int32)


_FAR = int(np.min(np.nonzero(_t5_bucket_np(-np.arange(0, 4 * MAX_DISTANCE)) == N_BUCKETS // 2 - 1)[0]))
assert np.all(_t5_bucket_np(-np.arange(_FAR, 1 << 16)) == N_BUCKETS // 2 - 1)
assert np.all(_t5_bucket_np(np.arange(_FAR, 1 << 16)) == N_BUCKETS - 1)
assert ATT_TILE >= _FAR


def _params(semantics, flags=None):
    return pltpu.CompilerParams(dimension_semantics=semantics, vmem_limit_bytes=VMEM_LIMIT_BYTES,
                                flags=flags)


def _rmsnorm_rows(x, g):
    ms = jnp.mean(x * x, axis=-1, keepdims=True)
    return x * lax.rsqrt(ms + EPS) * g


def _norm_into(x_ref, g_ref, h_ref, copy_ref=None):
    rows = x_ref.shape[0]

    def body(r, carry):
        r0 = pl.multiple_of(r * NORM_ROWS, NORM_ROWS)
        x = x_ref[pl.ds(r0, NORM_ROWS), :]
        h_ref[pl.ds(r0, NORM_ROWS), :] = _rmsnorm_rows(x, g_ref[...]).astype(h_ref.dtype)
        if copy_ref is not None:
            copy_ref[pl.ds(r0, NORM_ROWS), :] = x
        return carry

    lax.fori_loop(0, rows // NORM_ROWS, body, 0)


def _in_proj_kernel(x_ref, g_ref, w_ref, cs_ref, o_ref, h_ref):
    @pl.when(pl.program_id(1) == 0)
    def _():
        _norm_into(x_ref, g_ref, h_ref)

    acc = jnp.dot(h_ref[...], w_ref[...], preferred_element_type=F32)
    o_ref[...] = (acc * cs_ref[...]).astype(o_ref.dtype)


def _in_proj_col_scale():
    cs = np.ones((1, IN_WIDTH), np.float32)
    cs[:, :DIFF_WIDTH] = LOG2E * DIFF_QK_DIM ** -0.5
    cs[:, QW_COL:QW_COL + WIN_WIDTH] = LOG2E * HEAD_DIM ** -0.5
    return jnp.asarray(cs)


def _in_proj(x, g, w):
    n_tok, d = x.shape
    n_out = w.shape[1]
    tm = min(ROW_TILE_IN, n_tok)
    return pl.pallas_call(
        _in_proj_kernel,
        grid=(n_tok // tm, n_out // COL_TILE_IN),
        in_specs=[
            pl.BlockSpec((tm, d), lambda i, j: (i, 0)),
            pl.BlockSpec((1, d), lambda i, j: (0, 0)),
            pl.BlockSpec((d, COL_TILE_IN), lambda i, j: (0, j)),
            pl.BlockSpec((1, COL_TILE_IN), lambda i, j: (0, j)),
        ],
        out_specs=pl.BlockSpec((tm, COL_TILE_IN), lambda i, j: (i, j)),
        out_shape=jax.ShapeDtypeStruct((n_tok, n_out), BF16),
        scratch_shapes=[pltpu.VMEM((tm, d), BF16)],
        compiler_params=_params(("arbitrary", "arbitrary")),
        name="in_proj",
    )(x, g.reshape(1, d), w, _in_proj_col_scale())


def _diff_attn_kernel(cfar_ref, zero_ref, q_ref, k_ref, v_ref, bias_ref, lq1_ref, lk1_ref, lq2_ref, lk2_ref,
                      g_ref, o_ref, vt_ref, qt_ref, s00_ref, s01_ref, s10_ref, s11_ref,
                      p00_ref, p01_ref, p10_ref, p11_ref, smax_ref,
                      m_ref, alpha_ref, acc_ref, *, n_chunks, lambda_init):
    t = ATT_TILE
    h = pl.program_id(1)
    qi = pl.program_id(2)
    s_refs = ((s00_ref, s01_ref), (s10_ref, s11_ref))
    p_refs = ((p00_ref, p01_ref), (p10_ref, p11_ref))
    half_cols = (slice(0, t), slice(t, 2 * t))

    @pl.when(qi == 0)
    def _():
        def body(j, carry):
            r0 = pl.multiple_of(j * t, t)
            vt_ref[j, :HEAD_DIM] = v_ref[pl.ds(r0, t), :].astype(F32).T.astype(BF16)
            vt_ref[j, HEAD_DIM:] = jnp.ones((SUM_ROWS, t), BF16)
            return carry

        lax.fori_loop(0, n_chunks, body, 0)

    qt = q_ref[...].astype(F32).T
    row = lax.broadcasted_iota(jnp.int32, qt.shape, 0)
    qt_ref[:, :t] = jnp.where(row < DIFF_QK_DIM, qt, 0.0).astype(BF16)
    qt_ref[:, t:] = jnp.where(row >= DIFF_QK_DIM, qt, 0.0).astype(BF16)
    m_ref[...] = jnp.full(m_ref.shape, NEG, F32)
    acc_ref[...] = jnp.zeros(acc_ref.shape, F32)

    def chunk_of(r):
        return lax.rem(qi + r + (n_chunks - 1), n_chunks)

    def qk(j, half, par, tile):
        cols = half_cols[half]
        r0 = pl.multiple_of(j * t, t)
        s = jnp.dot(k_ref[pl.ds(r0, t), :], qt_ref[:, cols], preferred_element_type=F32)
        if tile is not None:
            s = s + bias_ref[tile]
        s_refs[half][par][...] = s
        smax_ref[par, :, cols] = jnp.max(s, axis=0, keepdims=True)

    def softmax(half, par, c):
        cols = half_cols[half]
        m_old = m_ref[:, cols]
        m_new = jnp.maximum(m_old, smax_ref[par, :, cols] + c)
        alpha = jnp.exp2(m_old - m_new)
        s = s_refs[half][par][pl.ds(pl.multiple_of(zero_ref[0], t), t), :]
        p_refs[half][par][...] = jnp.exp2(s - (m_new - c)).astype(BF16)
        alpha_ref[:, cols] = alpha
        m_ref[:, cols] = m_new

    def pv(j, half, par):
        cols = half_cols[half]
        acc_ref[:, cols] = alpha_ref[:, cols] * acc_ref[:, cols] + jnp.dot(
            vt_ref[j], p_refs[half][par][...], preferred_element_type=F32)

    def step(r, par, tile, next_tile, c, prev_c, first=False, last=False):
        j = chunk_of(r)
        qk(j, 1, par, tile)
        if not first:
            softmax(1, 1 - par, prev_c)
            pv(chunk_of(r - 1), 1, 1 - par)
        if not last:
            qk(chunk_of(r + 1), 0, 1 - par, next_tile)
        softmax(0, par, c)
        pv(j, 0, par)

    c_lo = cfar_ref[0, h]
    c_hi = cfar_ref[1, h]

    def far_c(r):
        return jnp.where(chunk_of(r) < qi, c_lo, c_hi)

    def far_step(r, par, prev_c=None, last=False):
        step(r, par, None, None, far_c(r), far_c(r - 1) if prev_c is None else prev_c, last=last)

    def far_group(i, carry):
        for u in range(FAR_UNROLL):
            far_step(4 + FAR_UNROLL * i + u, u % 2)
        return carry

    n_groups = (n_chunks - 4) // FAR_UNROLL
    near_tiles = (jnp.where(qi >= 1, 0, 3), 1, jnp.where(qi <= n_chunks - 2, 2, 4))
    qk(chunk_of(0), 0, 0, near_tiles[0])
    step(0, 0, near_tiles[0], near_tiles[1], 0.0, 0.0, first=True)
    step(1, 1, near_tiles[1], near_tiles[2], 0.0, 0.0)
    def middle():
        step(2, 0, near_tiles[2], None, 0.0, 0.0)
        far_step(3, 1, prev_c=0.0, last=(n_groups == 0))

    if n_groups > 2:
        pl.when(qi >= 0)(middle)
    else:
        middle()

    if n_groups > 0:
        lax.fori_loop(0, n_groups - 1, far_group, 0)
        for u in range(FAR_UNROLL):
            far_step(4 + FAR_UNROLL * (n_groups - 1) + u, u % 2, last=(u == FAR_UNROLL - 1))
    last_par = (n_chunks - 1) % 2
    softmax(1, last_par, far_c(n_chunks - 1))
    pv(chunk_of(n_chunks - 1), 1, last_par)

    inv_l = 1.0 / acc_ref[HEAD_DIM:HEAD_DIM + 1, :]
    acc = acc_ref[:HEAD_DIM, :]
    o0 = acc[:, :t] * inv_l[:, :t]
    o1 = acc[:, t:] * inv_l[:, t:]
    lam = (jnp.exp(jnp.sum(lq1_ref[...] * lk1_ref[...], axis=1, keepdims=True))
           - jnp.exp(jnp.sum(lq2_ref[...] * lk2_ref[...], axis=1, keepdims=True)) + lambda_init)
    ot = o0 - lam * o1
    ms = jnp.mean(ot * ot, axis=0, keepdims=True)
    y = ot * lax.rsqrt(ms + EPS) * g_ref[...] * (1.0 - lambda_init)
    o_ref[...] = y.T.astype(o_ref.dtype)


def _diff_attention(proj, batch, seq, cfar, bias_tiles, lq1, lk1, lq2, lk2, g, lambda_init):
    t = ATT_TILE
    nq = seq // t
    assert seq % t == 0 and nq >= 4 and (nq - 4) % FAR_UNROLL == 0, "4 leading chunks, then groups"
    vec = lambda a: a.reshape(1, DIFF_QK_DIM).astype(F32)
    vec_spec = pl.BlockSpec((1, DIFF_QK_DIM), lambda b, h, qi: (0, 0))
    kernel = functools.partial(_diff_attn_kernel, n_chunks=nq, lambda_init=lambda_init)
    return pl.pallas_call(
        kernel,
        grid=(batch, N_DIFF_HEADS, nq),
        in_specs=[
            pl.BlockSpec(memory_space=pltpu.SMEM),
            pl.BlockSpec(memory_space=pltpu.SMEM),
            pl.BlockSpec((t, HEAD_DIM), lambda b, h, qi: (b * nq + qi, QA_BLK + h)),
            pl.BlockSpec((seq, HEAD_DIM), lambda b, h, qi: (b, KA_BLK + h)),
            pl.BlockSpec((seq, HEAD_DIM), lambda b, h, qi: (b, VA_BLK + h)),
            pl.BlockSpec((None, N_BIAS_TILES, t, t), lambda b, h, qi: (h, 0, 0, 0)),
            vec_spec, vec_spec, vec_spec, vec_spec,
            pl.BlockSpec((HEAD_DIM, 1), lambda b, h, qi: (0, 0)),
        ],
        out_specs=pl.BlockSpec((t, HEAD_DIM), lambda b, h, qi: (b * nq + qi, h)),
        out_shape=jax.ShapeDtypeStruct((batch * seq, DIFF_WIDTH), BF16),
        scratch_shapes=[
            pltpu.VMEM((nq, HEAD_DIM + SUM_ROWS, t), BF16),
            pltpu.VMEM((HEAD_DIM, 2 * t), BF16),
            pltpu.VMEM((t, t), F32),
            pltpu.VMEM((t, t), F32),
            pltpu.VMEM((t, t), F32),
            pltpu.VMEM((t, t), F32),
            pltpu.VMEM((t, t), BF16),
            pltpu.VMEM((t, t), BF16),
            pltpu.VMEM((t, t), BF16),
            pltpu.VMEM((t, t), BF16),
            pltpu.VMEM((2, 1, 2 * t), F32),
            pltpu.VMEM((1, 2 * t), F32),
            pltpu.VMEM((1, 2 * t), F32),
            pltpu.VMEM((HEAD_DIM + SUM_ROWS, 2 * t), F32),
        ],
        compiler_params=_params(("arbitrary", "arbitrary", "arbitrary"), DIFF_ATTN_FLAGS),
        name="diff_attn",
    )(cfar, jnp.zeros((1,), jnp.int32), proj, proj, proj, bias_tiles, vec(lq1), vec(lk1), vec(lq2), vec(lk2),
      g.reshape(HEAD_DIM, 1).astype(F32))


def _win_attn_kernel(sink_ref, q_ref, *refs, seq):
    n_band = WIN_Q_BLOCKS + 2
    k_refs, v_refs = refs[:n_band], refs[n_band:2 * n_band]
    bias_ref, o_ref = refs[2 * n_band], refs[2 * n_band + 1]
    first = pl.program_id(1) * WIN_Q_BLOCKS

    def band(block_refs, u, kv):
        cols = slice(kv * HEAD_DIM, (kv + 1) * HEAD_DIM)
        return jnp.concatenate([r[:, cols] for r in block_refs[u:u + 3]], axis=0)

    def head_cols(kv, g):
        h = kv * WIN_GROUP + g
        return slice(h * HEAD_DIM, (h + 1) * HEAD_DIM)

    chains = [(u, kv) for u in range(WIN_Q_BLOCKS) for kv in range(N_WIN_KV)]
    scores = []
    for u, kv in chains:
        rows = slice(u * BLOCK, (u + 1) * BLOCK)
        q = jnp.concatenate([q_ref[rows, head_cols(kv, g)] for g in range(WIN_GROUP)], axis=0)
        scores.append(lax.dot_general(q, band(k_refs, u, kv), (((1,), (1,)), ((), ())),
                                      preferred_element_type=F32))
    for (u, kv), s in zip(chains, scores):
        key_pos = (first + u) * BLOCK - WINDOW + lax.broadcasted_iota(jnp.int32, (1, 3 * BLOCK), 1)
        s = jnp.where((key_pos >= 0) & (key_pos < seq), s, NEG) + bias_ref[kv]
        sink = jnp.concatenate([jnp.full((BLOCK, 1), sink_ref[kv * WIN_GROUP + g], F32)
                                for g in range(WIN_GROUP)], axis=0)
        m = jnp.maximum(jnp.max(s, axis=-1, keepdims=True), sink)
        e = jnp.exp2(s - m)
        den = jnp.sum(e, axis=-1, keepdims=True) + jnp.exp2(sink - m)
        o = jnp.dot(e.astype(BF16), band(v_refs, u, kv), preferred_element_type=F32) / den
        for g in range(WIN_GROUP):
            o_ref[u * BLOCK:(u + 1) * BLOCK, head_cols(kv, g)] = (
                o[g * BLOCK:(g + 1) * BLOCK, :].astype(o_ref.dtype))


def _window_attention(proj, batch, seq, sink, bias_tiles):
    nblk = seq // BLOCK
    steps = nblk // WIN_Q_BLOCKS
    assert nblk % WIN_Q_BLOCKS == 0

    def band_spec(d, blk):
        return pl.BlockSpec(
            (BLOCK, WIN_KV_WIDTH),
            lambda b, i: (b * nblk + jnp.clip(i * WIN_Q_BLOCKS + d, 0, nblk - 1), blk // N_WIN_KV))

    offsets = range(-1, WIN_Q_BLOCKS + 1)
    kernel = functools.partial(_win_attn_kernel, seq=seq)
    q_rows = WIN_Q_BLOCKS * BLOCK
    return pl.pallas_call(
        kernel,
        grid=(batch, steps),
        in_specs=[
            pl.BlockSpec(memory_space=pltpu.SMEM),
            pl.BlockSpec((q_rows, WIN_WIDTH), lambda b, i: (b * steps + i, QW_COL // WIN_WIDTH)),
            *[band_spec(d, KW_BLK) for d in offsets],
            *[band_spec(d, VW_BLK) for d in offsets],
            pl.BlockSpec((N_WIN_KV, WIN_GROUP * BLOCK, 3 * BLOCK), lambda b, i: (0, 0, 0)),
        ],
        out_specs=pl.BlockSpec((q_rows, WIN_WIDTH), lambda b, i: (b * steps + i, 0)),
        out_shape=jax.ShapeDtypeStruct((batch * seq, WIN_WIDTH), BF16),
        compiler_params=_params(("arbitrary", "arbitrary")),
        name="win_attn",
    )(sink.astype(F32) * LOG2E, proj, *([proj] * (2 * len(offsets))), bias_tiles)


def _out_proj_kernel(x_ref, oa_ref, ow_ref, wa_ref, ww_ref, o_ref):
    acc = jnp.dot(oa_ref[...], wa_ref[...], preferred_element_type=F32)
    acc = acc + jnp.dot(ow_ref[...], ww_ref[...], preferred_element_type=F32)
    o_ref[...] = x_ref[...] + acc


def _out_proj(x, oa, ow, wa, ww):
    n_tok, d = x.shape
    tm = min(ROW_TILE_OUT, n_tok)
    return pl.pallas_call(
        _out_proj_kernel,
        grid=(n_tok // tm,),
        in_specs=[
            pl.BlockSpec((tm, d), lambda i: (i, 0)),
            pl.BlockSpec((tm, DIFF_WIDTH), lambda i: (i, 0)),
            pl.BlockSpec((tm, WIN_WIDTH), lambda i: (i, 0)),
            pl.BlockSpec((DIFF_WIDTH, d), lambda i: (0, 0)),
            pl.BlockSpec((WIN_WIDTH, d), lambda i: (0, 0)),
        ],
        out_specs=pl.BlockSpec((tm, d), lambda i: (i, 0)),
        out_shape=jax.ShapeDtypeStruct((n_tok, d), F32),
        compiler_params=_params(("arbitrary",)),
        name="out_proj",
    )(x, oa, ow, wa, ww)


def _ffn_kernel(x_ref, g_ref, w1_ref, w2_ref, fg_ref, o_ref, h_ref, *, final_norm):
    f = pl.program_id(1)

    @pl.when(f == 0)
    def _():
        _norm_into(x_ref, g_ref, h_ref, copy_ref=o_ref)

    rows = h_ref.shape[0] // FF_ROW_GROUPS
    groups = [pl.ds(i * rows, rows) for i in range(FF_ROW_GROUPS)]
    hidden = [jnp.dot(h_ref[grp, :], w1_ref[...], preferred_element_type=F32) for grp in groups]
    for grp, a in zip(groups, hidden):
        a = jnp.square(jnp.maximum(a, 0.0)).astype(BF16)
        o_ref[grp, :] += jnp.dot(a, w2_ref[...], preferred_element_type=F32)

    if final_norm:
        @pl.when(f == pl.num_programs(1) - 1)
        def _():
            _norm_into(o_ref, fg_ref, o_ref)


def _ffn(x, g, w1, w2, final_g, final_norm):
    n_tok, d = x.shape
    d_ff = w1.shape[1]
    tm = min(ROW_TILE_FF, n_tok)
    kernel = functools.partial(_ffn_kernel, final_norm=final_norm)
    return pl.pallas_call(
        kernel,
        grid=(n_tok // tm, d_ff // FF_TILE),
        in_specs=[
            pl.BlockSpec((tm, d), lambda i, f: (i, 0)),
            pl.BlockSpec((1, d), lambda i, f: (0, 0)),
            pl.BlockSpec((d, FF_TILE), lambda i, f: (0, f)),
            pl.BlockSpec((FF_TILE, d), lambda i, f: (f, 0)),
            pl.BlockSpec((1, d), lambda i, f: (0, 0)),
        ],
        out_specs=pl.BlockSpec((tm, d), lambda i, f: (i, 0)),
        out_shape=jax.ShapeDtypeStruct((n_tok, d), F32),
        scratch_shapes=[pltpu.VMEM((tm, d), BF16)],
        compiler_params=_params(("arbitrary", "arbitrary")),
        name="ffn",
    )(x, g.reshape(1, d), w1, w2, final_g.reshape(1, d))


def _bucket_lookup(table, bucket):
    bucket = jnp.asarray(bucket.astype(np.int8))
    cols = table.T.astype(F32).reshape((table.shape[1],) + (1,) * bucket.ndim + (N_BUCKETS,))
    out = jnp.zeros((table.shape[1],) + bucket.shape, F32)
    for b in range(N_BUCKETS):
        out = jnp.where(bucket == b, cols[..., b], out)
    return out


def _diff_bias_tiles(table_a):
    t = ATT_TILE
    k = np.arange(t)[:, None]
    q = np.arange(t)[None, :]
    bucket = np.stack([_t5_bucket_np((o - 1) * t + k - q) for o in range(3)]
                      + [np.full((t, t), N_BUCKETS - 1), np.full((t, t), N_BUCKETS // 2 - 1)])
    assert bucket.shape[0] == N_BIAS_TILES
    return _bucket_lookup(table_a * LOG2E, bucket)


def _win_bias_tiles(table_b):
    rel = np.arange(3 * BLOCK)[None, :] - WINDOW - np.arange(BLOCK)[:, None]
    tiles = _bucket_lookup(table_b * LOG2E, _t5_bucket_np(rel))
    tiles = jnp.where(jnp.asarray(np.abs(rel) <= WINDOW), tiles, NEG)
    return tiles.reshape(N_WIN_KV, WIN_GROUP * BLOCK, 3 * BLOCK)


def _trunk(x3, diff_bias, win_bias, cfar, norm1_g, w_in, lambda_q1, lambda_k1, lambda_q2, lambda_k2,
           diff_subln_g, sink_logit, w_out, norm2_g, w_ff_in, w_ff_out, final_norm_g):
    batch, seq, d = x3.shape
    x = x3.reshape(batch * seq, d)
    for l in range(DEPTH):
        lambda_init = 0.8 - 0.6 * math.exp(-0.3 * l)
        proj = _in_proj(x, norm1_g[l], w_in[l])
        oa = _diff_attention(proj, batch, seq, cfar, diff_bias, lambda_q1[l], lambda_k1[l],
                             lambda_q2[l], lambda_k2[l], diff_subln_g[l], lambda_init)
        ow = _window_attention(proj, batch, seq, sink_logit[l], win_bias)
        x = _out_proj(x, oa, ow, w_out[l, :DIFF_WIDTH], w_out[l, DIFF_WIDTH:])
        x = _ffn(x, norm2_g[l], w_ff_in[l], w_ff_out[l], final_norm_g, final_norm=(l == DEPTH - 1))
    return x.reshape(batch, seq, d)


def kernel(x_prompt, x_sample, rel_bias, norm1_g, w_in, lambda_q1, lambda_k1, lambda_q2, lambda_k2,
           diff_subln_g, sink_logit, w_out, norm2_g, w_ff_in, w_ff_out, final_norm_g):
    table_a = rel_bias[:, :N_DIFF_HEADS]
    table_b = rel_bias[:, N_DIFF_HEADS:]
    diff_bias = _diff_bias_tiles(table_a)
    win_bias = _win_bias_tiles(table_b)
    cfar = jnp.stack([table_a[N_BUCKETS // 2 - 1], table_a[N_BUCKETS - 1]]).astype(F32) * LOG2E
    args = (diff_bias, win_bias, cfar, norm1_g, w_in.astype(BF16), lambda_q1, lambda_k1, lambda_q2,
            lambda_k2, diff_subln_g, sink_logit, w_out.astype(BF16), norm2_g, w_ff_in.astype(BF16),
            w_ff_out.astype(BF16), final_norm_g)
    return (_trunk(x_prompt, *args), _trunk(x_sample, *args))
```
